```python
import math
import jax, jax.numpy as jnp
from jax import lax
import numpy as np

D_MODEL = 2048
BATCH = 4
SEQ = 4096
DEPTH = 2

HEAD_DIM = 128
MIX_WIDTH = D_MODEL
ATTN_HEADS = (MIX_WIDTH // 2) // HEAD_DIM
ATTN_WIDTH = ATTN_HEADS * HEAD_DIM
GMLP_WIDTH = MIX_WIDTH - ATTN_WIDTH
GMLP_GROUPS = 8
GMLP_GROUP_DIM = GMLP_WIDTH // GMLP_GROUPS
CHUNK = 128
DILATED_BRANCHES = ((128, 1), (512, 4), (2048, 16))
ROPE_THETA = 10000.0
D_FF = 4 * D_MODEL
NORM_EPS = 1e-6
LN_EPS = 1e-5
IN_WIDTH = 3 * ATTN_WIDTH + 2 * GMLP_WIDTH

kernel_name = "hybrid_dilated_attn_gmlp_trunk"


def rms_norm(x, g):
    xf = x.astype(jnp.float32)
    y = xf * lax.rsqrt(jnp.mean(xf * xf, axis=-1, keepdims=True) + NORM_EPS)
    return (y * g.astype(jnp.float32)).astype(x.dtype)


def layer_norm(x, g, b):
    xf = x.astype(jnp.float32)
    mu = jnp.mean(xf, axis=-1, keepdims=True)
    xc = xf - mu
    var = jnp.mean(xc * xc, axis=-1, keepdims=True)
    y = xc * lax.rsqrt(var + LN_EPS) * g.astype(jnp.float32) + b.astype(jnp.float32)
    return y.astype(x.dtype)


def rotary(x, pos):
    half = HEAD_DIM // 2
    inv_freq = ROPE_THETA ** (-jnp.arange(half, dtype=jnp.float32) / half)
    ang = pos.astype(jnp.float32)[:, None] * inv_freq[None, :]
    cos = jnp.cos(ang)[None, :, None, :]
    sin = jnp.sin(ang)[None, :, None, :]
    xf = x.astype(jnp.float32)
    x1, x2 = xf[..., :half], xf[..., half:]
    return jnp.concatenate([x1 * cos - x2 * sin, x2 * cos + x1 * sin], axis=-1).astype(x.dtype)


def dilated_branch(q, k, v, window, dilation):
    B, S, H, Dh = q.shape
    blk = window // dilation
    span = dilation * blk
    s_pad = -(-S // span) * span
    nb = s_pad // span
    pad = ((0, 0), (0, s_pad - S), (0, 0), (0, 0))

    def blocks(t):
        return jnp.pad(t, pad).reshape(B, nb, blk, dilation, H, Dh)

    def with_prev(t):
        prev = jnp.concatenate([jnp.zeros_like(t[:, :1]), t[:, :-1]], axis=1)
        return jnp.concatenate([prev, t], axis=2)

    qb = blocks(q)
    kc = with_prev(blocks(k))
    vc = with_prev(blocks(v))
    scores = jnp.einsum('bnqrhd,bnkrhd->bnrhqk', qb, kc,
                        preferred_element_type=jnp.float32) * (Dh ** -0.5)
    qi = jnp.arange(blk)[:, None]
    kj = jnp.arange(2 * blk)[None, :]
    dist = qi - kj + blk
    key_idx = jnp.arange(nb)[:, None, None] * blk - blk + kj[None]
    valid = (dist >= 0)[None] & (dist <= blk)[None] & (key_idx >= 0)
    scores = jnp.where(valid[None, :, None, None], scores, -jnp.inf)
    m = jnp.max(scores, axis=-1, keepdims=True)
    p = jnp.exp(scores - m)
    l = jnp.sum(p, axis=-1, keepdims=True)
    o = jnp.einsum('bnrhqk,bnkrhd->bnqrhd', p / l, vc.astype(jnp.float32))
    lse = jnp.transpose((m + jnp.log(l))[..., 0], (0, 1, 4, 2, 3))
    o = o.reshape(B, s_pad, H, Dh)[:, :S]
    lse = lse.reshape(B, s_pad, H)[:, :S]
    return o, lse


def dilated_attention(q, k, v):
    outs, lses = [], []
    for window, dilation in DILATED_BRANCHES:
        o, lse = dilated_branch(q, k, v, window, dilation)
        outs.append(o)
        lses.append(lse)
    w = jax.nn.softmax(jnp.stack(lses, axis=0), axis=0)
    return jnp.einsum('nbsh,nbshd->bshd', w, jnp.stack(outs, axis=0))


def chunked_gmlp(u, v, ln_g, ln_b, w_s, b_s):
    B, S, _ = u.shape
    nc = S // CHUNK
    v = layer_norm(v.reshape(B, S, GMLP_GROUPS, GMLP_GROUP_DIM), ln_g, ln_b)
    v = v.reshape(B, nc, CHUNK, GMLP_GROUPS, GMLP_GROUP_DIM)
    causal = jnp.tril(jnp.ones((CHUNK, CHUNK), dtype=w_s.dtype))
    sp = jnp.einsum('gts,bcsgd->bctgd', w_s * causal[None], v) \
        + jnp.transpose(b_s)[None, None, :, :, None]
    out = u.reshape(B, nc, CHUNK, GMLP_GROUPS, GMLP_GROUP_DIM) * sp
    return out.reshape(B, S, GMLP_WIDTH)


def setup_inputs(seed: int = 0) -> dict:
    key = jax.random.key(seed)
    ks = jax.random.split(key, 13)
    f32 = jnp.float32
    nrm = lambda k, shape, s: jax.random.normal(k, shape, f32) * s
    return {
        "x": nrm(ks[0], (BATCH, SEQ, D_MODEL), 1.0),
        "norm1_g": 1.0 + nrm(ks[1], (DEPTH, D_MODEL), 0.02),
        "w_in": nrm(ks[2], (DEPTH, D_MODEL, IN_WIDTH), D_MODEL ** -0.5),
        "gmlp_ln_g": 1.0 + nrm(ks[3], (DEPTH, GMLP_GROUPS, GMLP_GROUP_DIM), 0.02),
        "gmlp_ln_b": nrm(ks[4], (DEPTH, GMLP_GROUPS, GMLP_GROUP_DIM), 0.02),
        "w_spatial": nrm(ks[5], (DEPTH, GMLP_GROUPS, CHUNK, CHUNK), CHUNK ** -0.5),
        "b_spatial": 1.0 + nrm(ks[6], (DEPTH, GMLP_GROUPS, CHUNK), 0.1),
        "w_out": nrm(ks[7], (DEPTH, MIX_WIDTH, D_MODEL), MIX_WIDTH ** -0.5),
        "norm2_g": 1.0 + nrm(ks[8], (DEPTH, D_MODEL), 0.02),
        "w_up": nrm(ks[9], (DEPTH, D_MODEL, D_FF), D_MODEL ** -0.5),
        "w_down": nrm(ks[10], (DEPTH, D_FF, D_MODEL), D_FF ** -0.5),
        "final_g": 1.0 + nrm(ks[11], (D_MODEL,), 0.02),
    }


def reference(x, norm1_g, w_in, gmlp_ln_g, gmlp_ln_b, w_spatial, b_spatial,
              w_out, norm2_g, w_up, w_down, final_g):
    B, S, _ = x.shape
    pos = jnp.arange(S, dtype=jnp.int32)
    splits = [ATTN_WIDTH, 2 * ATTN_WIDTH, 3 * ATTN_WIDTH, 3 * ATTN_WIDTH + GMLP_WIDTH]
    for l in range(DEPTH):
        h = rms_norm(x, norm1_g[l])
        z = h @ w_in[l]
        q, k, v, gu, gv = jnp.split(z, splits, axis=-1)
        heads = (B, S, ATTN_HEADS, HEAD_DIM)
        q = rotary(q.reshape(heads), pos)
        k = rotary(k.reshape(heads), pos)
        attn = dilated_attention(q, k, v.reshape(heads)).reshape(B, S, ATTN_WIDTH)
        gm = chunked_gmlp(jax.nn.gelu(gu), jax.nn.gelu(gv), gmlp_ln_g[l], gmlp_ln_b[l],
                          w_spatial[l], b_spatial[l])
        mix = jnp.concatenate([attn.astype(x.dtype), gm.astype(x.dtype)], axis=-1)
        x = x + mix @ w_out[l]
        h2 = rms_norm(x, norm2_g[l])
        x = x + jnp.square(jax.nn.relu(h2 @ w_up[l])) @ w_down[l]
    return rms_norm(x, final_g)
```

```python
import functools

import jax
import jax.numpy as jnp
from jax import lax
from jax.experimental import pallas as pl
from jax.experimental.pallas import tpu as pltpu

D_MODEL = 2048
HEAD_DIM = 128
ATTN_HEADS = 8
ATTN_WIDTH = ATTN_HEADS * HEAD_DIM
GMLP_GROUPS = 8
GMLP_GROUP_DIM = 128
GMLP_WIDTH = GMLP_GROUPS * GMLP_GROUP_DIM
CHUNK = 128
ATTN_BLOCK = 128
DILATIONS = (1, 4, 16)
ROPE_THETA = 10000.0
D_FF = 4 * D_MODEL
NORM_EPS = 1e-6
LN_EPS = 1e-5
IN_WIDTH = 3 * ATTN_WIDTH + 2 * GMLP_WIDTH

V7X_VMEM_LIMIT_BYTES = 56 * 1024 * 1024

IN_TM = 512
IN_TN = 1024
GMLP_TM = 512
OUT_TM = 1024
OUT_TN = 1024
MLP_TM = 512
MLP_TF = 1024
MERGE_ROWS = 256

F32 = jnp.float32
BF16 = jnp.bfloat16


def _rms_normalize(x, gain):
    return x * lax.rsqrt(jnp.mean(x * x, axis=-1, keepdims=True) + NORM_EPS) * gain


def _in_proj_kernel(x_ref, g_ref, w_ref, cos_ref, sin_ref, lng_ref, lnb_ref,
                    q1_ref, q4_ref, q16_ref, k1_ref, k4_ref, k16_ref,
                    v1_ref, v4_ref, v16_ref, gu_ref, gv_ref,
                    xn_ref, head_ref):
    j = pl.program_id(1)
    tm = x_ref.shape[0]

    @pl.when(j == 0)
    def _():
        xn_ref[...] = _rms_normalize(x_ref[...], g_ref[...]).astype(BF16)

    z = jnp.dot(xn_ref[...], w_ref[...], preferred_element_type=F32)

    def write_heads(nat_ref, d4_ref, d16_ref, rotate):
        for h in range(ATTN_HEADS):
            cols = slice(h * HEAD_DIM, (h + 1) * HEAD_DIM)
            y = z[:, cols]
            if rotate:
                y = y * cos_ref[...] + pltpu.roll(y, HEAD_DIM // 2, 1) * sin_ref[...]
            nat_ref[:, cols] = y.astype(BF16)
            head_ref[h] = y
            for d, dst in ((4, d4_ref), (16, d16_ref)):
                for r in range(d):
                    dst[r, :, cols] = head_ref[h, pl.ds(r, tm // d, stride=d), :].astype(BF16)

    @pl.when(j == 0)
    def _():
        write_heads(q1_ref, q4_ref, q16_ref, True)

    @pl.when(j == 1)
    def _():
        write_heads(k1_ref, k4_ref, k16_ref, True)

    @pl.when(j == 2)
    def _():
        write_heads(v1_ref, v4_ref, v16_ref, False)

    @pl.when(j == 3)
    def _():
        gu_ref[...] = jax.nn.gelu(z).astype(BF16)

    @pl.when(j == 4)
    def _():
        a = jax.nn.gelu(z)
        for g in range(GMLP_GROUPS):
            cols = slice(g * GMLP_GROUP_DIM, (g + 1) * GMLP_GROUP_DIM)
            ag = a[:, cols]
            mu = jnp.mean(ag, axis=-1, keepdims=True)
            xc = ag - mu
            var = jnp.mean(xc * xc, axis=-1, keepdims=True)
            y = xc * lax.rsqrt(var + LN_EPS) * lng_ref[:, cols] + lnb_ref[:, cols]
            gv_ref[:, cols] = y.astype(BF16)


def _in_proj(x2d, gain, w_bf16, cos_t, sin_t, ln_g, ln_b, batch, seq):
    m = x2d.shape[0]
    tm, tn = IN_TM, IN_TN
    nt = seq // tm
    grid = (m // tm, IN_WIDTH // tn)

    nat_shape = jax.ShapeDtypeStruct((m, ATTN_WIDTH), BF16)
    nat_spec = pl.BlockSpec((tm, ATTN_WIDTH), lambda i, j: (i, 0))

    def dil_shape(d):
        return jax.ShapeDtypeStruct((batch, d, seq // d, ATTN_WIDTH), BF16)

    def dil_spec(d):
        return pl.BlockSpec((None, d, tm // d, ATTN_WIDTH),
                            lambda i, j: (i // nt, 0, i % nt, 0))

    qkv_shapes = [nat_shape, dil_shape(4), dil_shape(16)] * 3
    qkv_specs = [nat_spec, dil_spec(4), dil_spec(16)] * 3
    g_shape = jax.ShapeDtypeStruct((m, GMLP_WIDTH), BF16)
    g_spec = pl.BlockSpec((tm, GMLP_WIDTH), lambda i, j: (i, 0))

    return pl.pallas_call(
        _in_proj_kernel,
        grid=grid,
        in_specs=[
            pl.BlockSpec((tm, D_MODEL), lambda i, j: (i, 0)),
            pl.BlockSpec((1, D_MODEL), lambda i, j: (0, 0)),
            pl.BlockSpec((D_MODEL, tn), lambda i, j: (0, j)),
            pl.BlockSpec((tm, HEAD_DIM), lambda i, j: (i % nt, 0)),
            pl.BlockSpec((tm, HEAD_DIM), lambda i, j: (i % nt, 0)),
            pl.BlockSpec((1, GMLP_WIDTH), lambda i, j: (0, 0)),
            pl.BlockSpec((1, GMLP_WIDTH), lambda i, j: (0, 0)),
        ],
        out_specs=qkv_specs + [g_spec, g_spec],
        out_shape=qkv_shapes + [g_shape, g_shape],
        scratch_shapes=[
            pltpu.VMEM((tm, D_MODEL), BF16),
            pltpu.VMEM((ATTN_HEADS, tm, HEAD_DIM), F32),
        ],
        compiler_params=pltpu.CompilerParams(
            dimension_semantics=("arbitrary", "arbitrary"),
            vmem_limit_bytes=V7X_VMEM_LIMIT_BYTES),
        name="in_proj",
    )(x2d, gain, w_bf16, cos_t, sin_t, ln_g, ln_b)


def _attention_kernel(q1_ref, k1_ref, v1_ref, q4_ref, k4_ref, v4_ref,
                      q16_ref, k16_ref, v16_ref, out_ref, o_ref, lse_ref):
    seq = out_ref.shape[0]
    blk = ATTN_BLOCK
    scale = HEAD_DIM ** -0.5
    row = lax.broadcasted_iota(jnp.int32, (blk, blk), 0)
    col = lax.broadcasted_iota(jnp.int32, (blk, blk), 1)
    cur_mask = col <= row
    prev_mask = col >= row
    nt_dims = (((1,), (1,)), ((), ()))

    branches = ((1, q1_ref, k1_ref, v1_ref),
                (4, q4_ref, k4_ref, v4_ref),
                (16, q16_ref, k16_ref, v16_ref))
    for br, (d, q_ref, k_ref, v_ref) in enumerate(branches):
        nb = seq // (d * blk)

        def body(idx, carry, d=d, nb=nb, br=br, q_ref=q_ref, k_ref=k_ref, v_ref=v_ref):
            n = idx % nb
            r = idx // nb
            cur = pl.ds(pl.multiple_of(idx * blk, blk), blk)
            prev = pl.ds(pl.multiple_of(jnp.maximum(idx - 1, 0) * blk, blk), blk)
            q = q_ref[cur, :]
            s_c = lax.dot_general(q, k_ref[cur, :], nt_dims, preferred_element_type=F32) * scale
            s_p = lax.dot_general(q, k_ref[prev, :], nt_dims, preferred_element_type=F32) * scale
            s_c = jnp.where(cur_mask, s_c, -jnp.inf)
            s_p = jnp.where(jnp.logical_and(prev_mask, n > 0), s_p, -jnp.inf)
            m = jnp.maximum(jnp.max(s_c, axis=-1, keepdims=True),
                            jnp.max(s_p, axis=-1, keepdims=True))
            p_c = jnp.exp(s_c - m)
            p_p = jnp.exp(s_p - m)
            l = jnp.sum(p_c, axis=-1, keepdims=True) + jnp.sum(p_p, axis=-1, keepdims=True)
            o = (jnp.dot(p_c.astype(BF16), v_ref[cur, :], preferred_element_type=F32)
                 + jnp.dot(p_p.astype(BF16), v_ref[prev, :], preferred_element_type=F32))
            o = o / l
            lse = jnp.broadcast_to(m + jnp.log(l), (blk, HEAD_DIM))
            start = n * (blk * d) + r
            if d == 1:
                rows = pl.ds(pl.multiple_of(start, blk), blk)
            else:
                rows = pl.ds(start, blk, stride=d)
            o_ref[br, rows, :] = o
            lse_ref[br, rows, :] = lse
            return carry

        lax.fori_loop(0, seq // blk, body, 0)

    def merge(c, carry):
        rows = pl.ds(pl.multiple_of(c * MERGE_ROWS, MERGE_ROWS), MERGE_ROWS)
        l0, l1, l2 = lse_ref[0, rows, :], lse_ref[1, rows, :], lse_ref[2, rows, :]
        top = jnp.maximum(jnp.maximum(l0, l1), l2)
        e0, e1, e2 = jnp.exp(l0 - top), jnp.exp(l1 - top), jnp.exp(l2 - top)
        inv = 1.0 / (e0 + e1 + e2)
        out = ((e0 * inv) * o_ref[0, rows, :] + (e1 * inv) * o_ref[1, rows, :]
               + (e2 * inv) * o_ref[2, rows, :])
        out_ref[rows, :] = out.astype(out_ref.dtype)
        return carry

    lax.fori_loop(0, seq // MERGE_ROWS, merge, 0)


def _attention(qkv, batch, seq):
    spec = pl.BlockSpec((None, seq, HEAD_DIM), lambda b, h: (b, 0, h))
    return pl.pallas_call(
        _attention_kernel,
        grid=(batch, ATTN_HEADS),
        in_specs=[spec] * 9,
        out_specs=spec,
        out_shape=jax.ShapeDtypeStruct((batch, seq, ATTN_WIDTH), BF16),
        scratch_shapes=[
            pltpu.VMEM((len(DILATIONS), seq, HEAD_DIM), F32),
            pltpu.VMEM((len(DILATIONS), seq, HEAD_DIM), F32),
        ],
        compiler_params=pltpu.CompilerParams(
            dimension_semantics=("arbitrary", "arbitrary"),
            vmem_limit_bytes=V7X_VMEM_LIMIT_BYTES),
        name="dilated_attention",
    )(*qkv)


def _gmlp_kernel(u_ref, v_ref, ws_ref, bs_ref, out_ref):
    tm = u_ref.shape[0]
    row = lax.broadcasted_iota(jnp.int32, (CHUNK, CHUNK), 0)
    col = lax.broadcasted_iota(jnp.int32, (CHUNK, CHUNK), 1)
    causal = col <= row
    for g in range(GMLP_GROUPS):
        cols = slice(g * GMLP_GROUP_DIM, (g + 1) * GMLP_GROUP_DIM)
        w = jnp.where(causal, ws_ref[g], 0.0).astype(BF16)
        bias = bs_ref[:, g:g + 1]
        for c in range(tm // CHUNK):
            rows = slice(c * CHUNK, (c + 1) * CHUNK)
            sp = jnp.dot(w, v_ref[rows, cols], preferred_element_type=F32) + bias
            out_ref[rows, cols] = (u_ref[rows, cols].astype(F32) * sp).astype(out_ref.dtype)


def _gmlp(gu, gv, w_spatial, b_spatial_t):
    m = gu.shape[0]
    tm = GMLP_TM
    spec = pl.BlockSpec((tm, GMLP_WIDTH), lambda i: (i, 0))
    return pl.pallas_call(
        _gmlp_kernel,
        grid=(m // tm,),
        in_specs=[spec, spec,
                  pl.BlockSpec((GMLP_GROUPS, CHUNK, CHUNK), lambda i: (0, 0, 0)),
                  pl.BlockSpec((CHUNK, GMLP_GROUPS), lambda i: (0, 0))],
        out_specs=spec,
        out_shape=jax.ShapeDtypeStruct((m, GMLP_WIDTH), BF16),
        compiler_params=pltpu.CompilerParams(
            dimension_semantics=("arbitrary",),
            vmem_limit_bytes=V7X_VMEM_LIMIT_BYTES),
        name="gmlp_spatial",
    )(gu, gv, w_spatial, b_spatial_t)


def _out_proj_kernel(attn_ref, gm_ref, wa_ref, wg_ref, x_ref, out_ref):
    out_ref[...] = (x_ref[...]
                    + jnp.dot(attn_ref[...], wa_ref[...], preferred_element_type=F32)
                    + jnp.dot(gm_ref[...], wg_ref[...], preferred_element_type=F32))


def _out_proj(attn, gm, w_out_bf16, x2d):
    m = x2d.shape[0]
    tm, tn = OUT_TM, OUT_TN
    gm_row_block = ATTN_WIDTH // GMLP_WIDTH
    return pl.pallas_call(
        _out_proj_kernel,
        grid=(m // tm, D_MODEL // tn),
        in_specs=[
            pl.BlockSpec((tm, ATTN_WIDTH), lambda i, j: (i, 0)),
            pl.BlockSpec((tm, GMLP_WIDTH), lambda i, j: (i, 0)),
            pl.BlockSpec((ATTN_WIDTH, tn), lambda i, j: (0, j)),
            pl.BlockSpec((GMLP_WIDTH, tn), lambda i, j: (gm_row_block, j)),
            pl.BlockSpec((tm, tn), lambda i, j: (i, j)),
        ],
        out_specs=pl.BlockSpec((tm, tn), lambda i, j: (i, j)),
        out_shape=jax.ShapeDtypeStruct((m, D_MODEL), F32),
        compiler_params=pltpu.CompilerParams(
            dimension_semantics=("arbitrary", "arbitrary"),
            vmem_limit_bytes=V7X_VMEM_LIMIT_BYTES),
        name="out_proj",
    )(attn, gm, w_out_bf16, w_out_bf16, x2d)


def _mlp_kernel(x_ref, g_ref, wu_ref, wd_ref, fg_ref, out_ref, xn_ref, *, final_norm):
    f = pl.program_id(1)

    @pl.when(f == 0)
    def _():
        x = x_ref[...]
        xn_ref[...] = _rms_normalize(x, g_ref[...]).astype(BF16)
        out_ref[...] = x

    h = jnp.dot(xn_ref[...], wu_ref[...], preferred_element_type=F32)
    h = jnp.square(jnp.maximum(h, 0.0)).astype(BF16)
    out_ref[...] += jnp.dot(h, wd_ref[...], preferred_element_type=F32)

    if final_norm:
        @pl.when(f == pl.num_programs(1) - 1)
        def _():
            out_ref[...] = _rms_normalize(out_ref[...], fg_ref[...])


def _mlp(x2d, gain, w_up_bf16, w_down_bf16, final_gain, final_norm):
    m = x2d.shape[0]
    tm, tf = MLP_TM, MLP_TF
    return pl.pallas_call(
        functools.partial(_mlp_kernel, final_norm=final_norm),
        grid=(m // tm, D_FF // tf),
        in_specs=[
            pl.BlockSpec((tm, D_MODEL), lambda i, f: (i, 0)),
            pl.BlockSpec((1, D_MODEL), lambda i, f: (0, 0)),
            pl.BlockSpec((D_MODEL, tf), lambda i, f: (0, f)),
            pl.BlockSpec((tf, D_MODEL), lambda i, f: (f, 0)),
            pl.BlockSpec((1, D_MODEL), lambda i, f: (0, 0)),
        ],
        out_specs=pl.BlockSpec((tm, D_MODEL), lambda i, f: (i, 0)),
        out_shape=jax.ShapeDtypeStruct((m, D_MODEL), F32),
        scratch_shapes=[pltpu.VMEM((tm, D_MODEL), BF16)],
        compiler_params=pltpu.CompilerParams(
            dimension_semantics=("arbitrary", "arbitrary"),
            vmem_limit_bytes=V7X_VMEM_LIMIT_BYTES),
        name="mlp",
    )(x2d, gain, w_up_bf16, w_down_bf16, final_gain)


def _rotary_tables(seq):
    half = HEAD_DIM // 2
    inv_freq = ROPE_THETA ** (-jnp.arange(half, dtype=F32) / half)
    ang = jnp.arange(seq, dtype=jnp.int32).astype(F32)[:, None] * inv_freq[None, :]
    cos, sin = jnp.cos(ang), jnp.sin(ang)
    return jnp.concatenate([cos, cos], axis=-1), jnp.concatenate([-sin, sin], axis=-1)


def kernel(x, norm1_g, w_in, gmlp_ln_g, gmlp_ln_b, w_spatial, b_spatial, w_out,
           norm2_g, w_up, w_down, final_g):
    batch, seq, _ = x.shape
    depth = w_in.shape[0]
    m = batch * seq
    cos_t, sin_t = _rotary_tables(seq)
    final_gain = final_g.reshape(1, D_MODEL)

    x2d = x.reshape(m, D_MODEL)
    for l in range(depth):
        outs = _in_proj(x2d, norm1_g[l].reshape(1, D_MODEL), w_in[l].astype(BF16),
                        cos_t, sin_t,
                        gmlp_ln_g[l].reshape(1, GMLP_WIDTH), gmlp_ln_b[l].reshape(1, GMLP_WIDTH),
                        batch, seq)
        qkv = [outs[3 * t + b].reshape(batch, seq, ATTN_WIDTH)
               for b in range(len(DILATIONS)) for t in range(3)]
        gu, gv = outs[9], outs[10]
        attn = _attention(qkv, batch, seq).reshape(m, ATTN_WIDTH)
        gm = _gmlp(gu, gv, w_spatial[l], jnp.transpose(b_spatial[l]))
        x2d = _out_proj(attn, gm, w_out[l].astype(BF16), x2d)
        x2d = _mlp(x2d, norm2_g[l].reshape(1, D_MODEL), w_up[l].astype(BF16),
                   w_down[l].astype(BF16), final_gain, final_norm=(l == depth - 1))
    return x2d.reshape(batch, seq, D_MODEL)
```

```python
import functools
import math

import jax
import jax.numpy as jnp
from jax import lax
from jax.experimental import pallas as pl
from jax.experimental.pallas import tpu as pltpu

D_MODEL = 2048
HEAD_DIM = 128
ATTN_HEADS = 8
ATTN_WIDTH = ATTN_HEADS * HEAD_DIM
GMLP_GROUPS = 8
GMLP_GROUP_DIM = 128
GMLP_WIDTH = GMLP_GROUPS * GMLP_GROUP_DIM
CHUNK = 128
ATTN_BLOCK = 128
DILATIONS = (1, 4, 16)
ROPE_THETA = 10000.0
D_FF = 4 * D_MODEL
NORM_EPS = 1e-6
LN_EPS = 1e-5

V7X_VMEM_LIMIT_BYTES = 56 * 1024 * 1024

NORM_TM = 1024
PROJ_TM = 512
GMLP_TM = 512
OUT_TM = 512
MLP_TM = 512
MLP_TF = 1024
ATTN_UNROLL = 8
MERGE_ROWS = 256

F32 = jnp.float32
BF16 = jnp.bfloat16


def _rms_normalize(x, gain):
    return x * lax.rsqrt(jnp.mean(x * x, axis=-1, keepdims=True) + NORM_EPS) * gain


def _params(semantics):
    return pltpu.CompilerParams(dimension_semantics=semantics,
                                vmem_limit_bytes=V7X_VMEM_LIMIT_BYTES)


def _resident(shape, index_map):
    return pl.BlockSpec(shape, index_map, pipeline_mode=pl.Buffered(1))


def _norm_kernel(x_ref, g_ref, out_ref):
    out_ref[...] = _rms_normalize(x_ref[...], g_ref[...]).astype(out_ref.dtype)


def _norm(x2d, gain):
    m = x2d.shape[0]
    spec = pl.BlockSpec((NORM_TM, D_MODEL), lambda i: (i, 0))
    return pl.pallas_call(
        _norm_kernel,
        grid=(m // NORM_TM,),
        in_specs=[spec, pl.BlockSpec((1, D_MODEL), lambda i: (0, 0))],
        out_specs=spec,
        out_shape=jax.ShapeDtypeStruct((m, D_MODEL), BF16),
        compiler_params=_params(("arbitrary",)),
        name="input_norm",
    )(x2d, gain)


def _qkv_proj_kernel(xn_ref, wq_ref, wk_ref, wv_ref, cos_ref, sin_ref,
                     q1_ref, q4_ref, q16_ref, k1_ref, k4_ref, k16_ref,
                     v1_ref, v4_ref, v16_ref, qh_ref, kh_ref, vh_ref):
    tm = xn_ref.shape[0]
    xn = xn_ref[...]
    jobs = ((wq_ref, True, q1_ref, q4_ref, q16_ref, qh_ref),
            (wk_ref, True, k1_ref, k4_ref, k16_ref, kh_ref),
            (wv_ref, False, v1_ref, v4_ref, v16_ref, vh_ref))
    for w_ref, rotate, nat_ref, d4_ref, d16_ref, head_ref in jobs:
        z = jnp.dot(xn, w_ref[...], preferred_element_type=F32)
        for h in range(ATTN_HEADS):
            cols = slice(h * HEAD_DIM, (h + 1) * HEAD_DIM)
            y = z[:, cols]
            if rotate:
                y = y * cos_ref[...] + pltpu.roll(y, HEAD_DIM // 2, 1) * sin_ref[...]
            nat_ref[:, cols] = y.astype(BF16)
            head_ref[h] = y
            for d, dst in ((4, d4_ref), (16, d16_ref)):
                for r in range(d):
                    dst[r, :, cols] = head_ref[h, pl.ds(r, tm // d, stride=d), :].astype(BF16)


def _qkv_proj(xn, w_in_bf16, cos_t, sin_t, batch, seq):
    m = xn.shape[0]
    tm = PROJ_TM
    nt = seq // tm

    nat_shape = jax.ShapeDtypeStruct((m, ATTN_WIDTH), BF16)
    nat_spec = pl.BlockSpec((tm, ATTN_WIDTH), lambda i: (i, 0))

    def dil_shape(d):
        return jax.ShapeDtypeStruct((batch, d, seq // d, ATTN_WIDTH), BF16)

    def dil_spec(d):
        return pl.BlockSpec((None, d, tm // d, ATTN_WIDTH), lambda i: (i // nt, 0, i % nt, 0))

    def w_spec(col_block):
        return _resident((D_MODEL, ATTN_WIDTH), lambda i: (0, col_block))

    table_spec = pl.BlockSpec((tm, HEAD_DIM), lambda i: (i % nt, 0))
    head_scratch = pltpu.VMEM((ATTN_HEADS, tm, HEAD_DIM), F32)
    return pl.pallas_call(
        _qkv_proj_kernel,
        grid=(m // tm,),
        in_specs=[pl.BlockSpec((tm, D_MODEL), lambda i: (i, 0)),
                  w_spec(0), w_spec(1), w_spec(2), table_spec, table_spec],
        out_specs=[nat_spec, dil_spec(4), dil_spec(16)] * 3,
        out_shape=[nat_shape, dil_shape(4), dil_shape(16)] * 3,
        scratch_shapes=[head_scratch] * 3,
        compiler_params=_params(("arbitrary",)),
        name="qkv_proj",
    )(xn, w_in_bf16, w_in_bf16, w_in_bf16, cos_t, sin_t)


def _gate_proj_kernel(xn_ref, wu_ref, wv_ref, lng_ref, lnb_ref, gu_ref, gv_ref):
    xn = xn_ref[...]
    gu_ref[...] = jax.nn.gelu(jnp.dot(xn, wu_ref[...], preferred_element_type=F32)).astype(BF16)
    a = jax.nn.gelu(jnp.dot(xn, wv_ref[...], preferred_element_type=F32))
    for g in range(GMLP_GROUPS):
        cols = slice(g * GMLP_GROUP_DIM, (g + 1) * GMLP_GROUP_DIM)
        ag = a[:, cols]
        mu = jnp.mean(ag, axis=-1, keepdims=True)
        xc = ag - mu
        var = jnp.mean(xc * xc, axis=-1, keepdims=True)
        y = xc * lax.rsqrt(var + LN_EPS) * lng_ref[:, cols] + lnb_ref[:, cols]
        gv_ref[:, cols] = y.astype(BF16)


def _gate_proj(xn, w_in_bf16, ln_g, ln_b):
    m = xn.shape[0]
    tm = PROJ_TM
    first_block = 3 * ATTN_WIDTH // GMLP_WIDTH
    out_spec = pl.BlockSpec((tm, GMLP_WIDTH), lambda i: (i, 0))
    out_shape = jax.ShapeDtypeStruct((m, GMLP_WIDTH), BF16)
    vec_spec = pl.BlockSpec((1, GMLP_WIDTH), lambda i: (0, 0))
    return pl.pallas_call(
        _gate_proj_kernel,
        grid=(m // tm,),
        in_specs=[pl.BlockSpec((tm, D_MODEL), lambda i: (i, 0)),
                  _resident((D_MODEL, GMLP_WIDTH), lambda i: (0, first_block)),
                  _resident((D_MODEL, GMLP_WIDTH), lambda i: (0, first_block + 1)),
                  vec_spec, vec_spec],
        out_specs=[out_spec, out_spec],
        out_shape=[out_shape, out_shape],
        compiler_params=_params(("arbitrary",)),
        name="gate_proj",
    )(xn, w_in_bf16, w_in_bf16, ln_g, ln_b)


def _attention_kernel(q1_ref, k1_ref, v1_ref, q4_ref, k4_ref, v4_ref,
                      q16_ref, k16_ref, v16_ref, out_ref, acc_ref, max_ref, sum_ref):
    seq = out_ref.shape[0]
    blk = ATTN_BLOCK
    exp2_scale = HEAD_DIM ** -0.5 * math.log2(math.e)
    row = lax.broadcasted_iota(jnp.int32, (blk, blk), 0)
    col = lax.broadcasted_iota(jnp.int32, (blk, blk), 1)
    cur_mask = col <= row
    prev_mask = col >= row
    nt_dims = (((1,), (1,)), ((), ()))
    ones = jnp.ones((blk, HEAD_DIM), BF16)

    branches = ((1, q1_ref, k1_ref, v1_ref),
                (4, q4_ref, k4_ref, v4_ref),
                (16, q16_ref, k16_ref, v16_ref))
    for br, (d, q_ref, k_ref, v_ref) in enumerate(branches):
        nb = seq // (d * blk)

        def body(idx, carry, d=d, nb=nb, br=br, q_ref=q_ref, k_ref=k_ref, v_ref=v_ref):
            n = idx % nb
            r = idx // nb
            cur = pl.ds(pl.multiple_of(idx * blk, blk), blk)
            prev = pl.ds(pl.multiple_of(jnp.maximum(idx - 1, 0) * blk, blk), blk)
            q = q_ref[cur, :]
            s_c = lax.dot_general(q, k_ref[cur, :], nt_dims, preferred_element_type=F32)
            s_p = lax.dot_general(q, k_ref[prev, :], nt_dims, preferred_element_type=F32)
            s_c = jnp.where(cur_mask, s_c, -jnp.inf)
            s_p = jnp.where(jnp.logical_and(prev_mask, n > 0), s_p, -jnp.inf)
            m = jnp.max(jnp.maximum(s_c, s_p), axis=-1, keepdims=True)
            p_c = jnp.exp2((s_c - m) * exp2_scale).astype(BF16)
            p_p = jnp.exp2((s_p - m) * exp2_scale).astype(BF16)
            v_c = jnp.concatenate([v_ref[cur, :], ones], axis=1)
            v_p = jnp.concatenate([v_ref[prev, :], ones], axis=1)
            acc = (jnp.dot(p_c, v_c, preferred_element_type=F32)
                   + jnp.dot(p_p, v_p, preferred_element_type=F32))
            start = n * (blk * d) + r
            if d == 1:
                rows = pl.ds(pl.multiple_of(start, blk), blk)
            else:
                rows = pl.ds(start, blk, stride=d)
            acc_ref[br, rows, :] = acc[:, :HEAD_DIM]
            sum_ref[br, rows, :] = acc[:, HEAD_DIM:]
            max_ref[br, rows, :] = jnp.broadcast_to(m, (blk, HEAD_DIM))
            return carry

        lax.fori_loop(0, seq // blk, body, 0, unroll=ATTN_UNROLL)

    def merge(c, carry):
        rows = pl.ds(pl.multiple_of(c * MERGE_ROWS, MERGE_ROWS), MERGE_ROWS)
        m0, m1, m2 = max_ref[0, rows, :], max_ref[1, rows, :], max_ref[2, rows, :]
        top = jnp.maximum(jnp.maximum(m0, m1), m2)
        e0 = jnp.exp2((m0 - top) * exp2_scale)
        e1 = jnp.exp2((m1 - top) * exp2_scale)
        e2 = jnp.exp2((m2 - top) * exp2_scale)
        den = e0 * sum_ref[0, rows, :] + e1 * sum_ref[1, rows, :] + e2 * sum_ref[2, rows, :]
        num = e0 * acc_ref[0, rows, :] + e1 * acc_ref[1, rows, :] + e2 * acc_ref[2, rows, :]
        out_ref[rows, :] = (num / den).astype(out_ref.dtype)
        return carry

    lax.fori_loop(0, seq // MERGE_ROWS, merge, 0)


def _attention(qkv, batch, seq):
    spec = pl.BlockSpec((None, seq, HEAD_DIM), lambda b, h: (b, 0, h))
    stat = pltpu.VMEM((len(DILATIONS), seq, HEAD_DIM), F32)
    return pl.pallas_call(
        _attention_kernel,
        grid=(batch, ATTN_HEADS),
        in_specs=[spec] * 9,
        out_specs=spec,
        out_shape=jax.ShapeDtypeStruct((batch, seq, ATTN_WIDTH), BF16),
        scratch_shapes=[stat, stat, stat],
        compiler_params=_params(("arbitrary", "arbitrary")),
        name="dilated_attention",
    )(*qkv)


def _gmlp_kernel(u_ref, v_ref, ws_ref, bs_ref, out_ref):
    tm = u_ref.shape[0]
    row = lax.broadcasted_iota(jnp.int32, (CHUNK, CHUNK), 0)
    col = lax.broadcasted_iota(jnp.int32, (CHUNK, CHUNK), 1)
    causal = col <= row
    for g in range(GMLP_GROUPS):
        cols = slice(g * GMLP_GROUP_DIM, (g + 1) * GMLP_GROUP_DIM)
        w = jnp.where(causal, ws_ref[g], 0.0).astype(BF16)
        bias = bs_ref[:, g:g + 1]
        for c in range(tm // CHUNK):
            rows = slice(c * CHUNK, (c + 1) * CHUNK)
            sp = jnp.dot(w, v_ref[rows, cols], preferred_element_type=F32) + bias
            out_ref[rows, cols] = (u_ref[rows, cols].astype(F32) * sp).astype(out_ref.dtype)


def _gmlp(gu, gv, w_spatial, b_spatial_t):
    m = gu.shape[0]
    tm = GMLP_TM
    spec = pl.BlockSpec((tm, GMLP_WIDTH), lambda i: (i, 0))
    return pl.pallas_call(
        _gmlp_kernel,
        grid=(m // tm,),
        in_specs=[spec, spec,
                  pl.BlockSpec((GMLP_GROUPS, CHUNK, CHUNK), lambda i: (0, 0, 0)),
                  pl.BlockSpec((CHUNK, GMLP_GROUPS), lambda i: (0, 0))],
        out_specs=spec,
        out_shape=jax.ShapeDtypeStruct((m, GMLP_WIDTH), BF16),
        compiler_params=_params(("arbitrary",)),
        name="gmlp_spatial",
    )(gu, gv, w_spatial, b_spatial_t)


def _out_proj_kernel(attn_ref, gm_ref, wa_ref, wg_ref, x_ref, g_ref, x1_ref, xn_ref):
    x1 = (x_ref[...]
          + jnp.dot(attn_ref[...], wa_ref[...], preferred_element_type=F32)
          + jnp.dot(gm_ref[...], wg_ref[...], preferred_element_type=F32))
    x1_ref[...] = x1
    xn_ref[...] = _rms_normalize(x1, g_ref[...]).astype(BF16)


def _out_proj(attn, gm, w_out_bf16, x2d, gain):
    m = x2d.shape[0]
    tm = OUT_TM
    gm_row_block = ATTN_WIDTH // GMLP_WIDTH
    row_spec = pl.BlockSpec((tm, D_MODEL), lambda i: (i, 0))
    return pl.pallas_call(
        _out_proj_kernel,
        grid=(m // tm,),
        in_specs=[
            pl.BlockSpec((tm, ATTN_WIDTH), lambda i: (i, 0)),
            pl.BlockSpec((tm, GMLP_WIDTH), lambda i: (i, 0)),
            _resident((ATTN_WIDTH, D_MODEL), lambda i: (0, 0)),
            _resident((GMLP_WIDTH, D_MODEL), lambda i: (gm_row_block, 0)),
            row_spec,
            pl.BlockSpec((1, D_MODEL), lambda i: (0, 0)),
        ],
        out_specs=[row_spec, row_spec],
        out_shape=[jax.ShapeDtypeStruct((m, D_MODEL), F32),
                   jax.ShapeDtypeStruct((m, D_MODEL), BF16)],
        compiler_params=_params(("arbitrary",)),
        name="out_proj",
    )(attn, gm, w_out_bf16, w_out_bf16, x2d, gain)


def _mlp_kernel(x_ref, xn_ref, wu_ref, wd_ref, g_ref, *out_refs, last_layer):
    f = pl.program_id(1)
    acc_ref = out_refs[0]

    @pl.when(f == 0)
    def _():
        acc_ref[...] = x_ref[...]

    h = jnp.dot(xn_ref[...], wu_ref[...], preferred_element_type=F32)
    h = jnp.square(jnp.maximum(h, 0.0)).astype(BF16)
    acc_ref[...] += jnp.dot(h, wd_ref[...], preferred_element_type=F32)

    @pl.when(f == pl.num_programs(1) - 1)
    def _():
        normed = _rms_normalize(acc_ref[...], g_ref[...])
        if last_layer:
            acc_ref[...] = normed
        else:
            out_refs[1][...] = normed.astype(BF16)


def _mlp(x1, xn, w_up_bf16, w_down_bf16, next_gain, last_layer):
    m = x1.shape[0]
    tm, tf = MLP_TM, MLP_TF
    row_spec = pl.BlockSpec((tm, D_MODEL), lambda i, f: (i, 0))
    out_specs = [row_spec]
    out_shape = [jax.ShapeDtypeStruct((m, D_MODEL), F32)]
    if not last_layer:
        out_specs.append(row_spec)
        out_shape.append(jax.ShapeDtypeStruct((m, D_MODEL), BF16))
    return pl.pallas_call(
        functools.partial(_mlp_kernel, last_layer=last_layer),
        grid=(m // tm, D_FF // tf),
        in_specs=[
            row_spec,
            row_spec,
            pl.BlockSpec((D_MODEL, tf), lambda i, f: (0, f)),
            pl.BlockSpec((tf, D_MODEL), lambda i, f: (f, 0)),
            pl.BlockSpec((1, D_MODEL), lambda i, f: (0, 0)),
        ],
        out_specs=out_specs,
        out_shape=out_shape,
        compiler_params=_params(("arbitrary", "arbitrary")),
        name="mlp",
    )(x1, xn, w_up_bf16, w_down_bf16, next_gain)


def _rotary_tables(seq):
    half = HEAD_DIM // 2
    inv_freq = ROPE_THETA ** (-jnp.arange(half, dtype=F32) / half)
    ang = jnp.arange(seq, dtype=jnp.int32).astype(F32)[:, None] * inv_freq[None, :]
    cos, sin = jnp.cos(ang), jnp.sin(ang)
    return jnp.concatenate([cos, cos], axis=-1), jnp.concatenate([-sin, sin], axis=-1)


def kernel(x, norm1_g, w_in, gmlp_ln_g, gmlp_ln_b, w_spatial, b_spatial, w_out,
           norm2_g, w_up, w_down, final_g):
    batch, seq, _ = x.shape
    depth = w_in.shape[0]
    m = batch * seq
    cos_t, sin_t = _rotary_tables(seq)
    gains1 = norm1_g.reshape(depth, 1, D_MODEL)
    gains2 = norm2_g.reshape(depth, 1, D_MODEL)
    final_gain = final_g.reshape(1, D_MODEL)

    x2d = x.reshape(m, D_MODEL)
    xn = _norm(x2d, gains1[0])
    for l in range(depth):
        last_layer = l == depth - 1
        w_in_l = w_in[l].astype(BF16)
        outs = _qkv_proj(xn, w_in_l, cos_t, sin_t, batch, seq)
        qkv = [outs[3 * t + b].reshape(batch, seq, ATTN_WIDTH)
               for b in range(len(DILATIONS)) for t in range(3)]
        gu, gv = _gate_proj(xn, w_in_l, gmlp_ln_g[l].reshape(1, GMLP_WIDTH),
                            gmlp_ln_b[l].reshape(1, GMLP_WIDTH))
        attn = _attention(qkv, batch, seq).reshape(m, ATTN_WIDTH)
        gm = _gmlp(gu, gv, w_spatial[l], jnp.transpose(b_spatial[l]))
        x1, xn2 = _out_proj(attn, gm, w_out[l].astype(BF16), x2d, gains2[l])
        next_gain = final_gain if last_layer else gains1[l + 1]
        res = _mlp(x1, xn2, w_up[l].astype(BF16), w_down[l].astype(BF16), next_gain, last_layer)
        if last_layer:
            x2d = res[0]
        else:
            x2d, xn = res
    return x2d.reshape(batch, seq, D_MODEL)
```

```python
import functools
import math

import jax
import jax.numpy as jnp
from jax import lax
from jax.experimental import pallas as pl
from jax.experimental.pallas import tpu as pltpu

D_MODEL = 2048
HEAD_DIM = 128
ATTN_HEADS = 8
ATTN_WIDTH = ATTN_HEADS * HEAD_DIM
GMLP_GROUPS = 8
GMLP_GROUP_DIM = 128
GMLP_WIDTH = GMLP_GROUPS * GMLP_GROUP_DIM
CHUNK = 128
ATTN_BLOCK = 128
DILATIONS = (1, 4, 16)
ROPE_THETA = 10000.0
D_FF = 4 * D_MODEL
NORM_EPS = 1e-6
LN_EPS = 1e-5

V7X_VMEM_LIMIT_BYTES = 56 * 1024 * 1024

NORM_TM = 1024
PROJ_TM = 512
GMLP_TM = 1024
OUT_TM = 512
MLP_TM = 512
MLP_TF = 1024
ATTN_UNROLL = 16
MERGE_ROWS = 256

F32 = jnp.float32
BF16 = jnp.bfloat16


def _rms_normalize(x, gain):
    return x * lax.rsqrt(jnp.mean(x * x, axis=-1, keepdims=True) + NORM_EPS) * gain


def _params(semantics):
    return pltpu.CompilerParams(dimension_semantics=semantics,
                                vmem_limit_bytes=V7X_VMEM_LIMIT_BYTES)


def _resident(shape, index_map):
    return pl.BlockSpec(shape, index_map, pipeline_mode=pl.Buffered(1))


def _norm_kernel(x_ref, g_ref, out_ref):
    out_ref[...] = _rms_normalize(x_ref[...], g_ref[...]).astype(out_ref.dtype)


def _norm(x2d, gain):
    m = x2d.shape[0]
    spec = pl.BlockSpec((NORM_TM, D_MODEL), lambda i: (i, 0))
    return pl.pallas_call(
        _norm_kernel,
        grid=(m // NORM_TM,),
        in_specs=[spec, pl.BlockSpec((1, D_MODEL), lambda i: (0, 0))],
        out_specs=spec,
        out_shape=jax.ShapeDtypeStruct((m, D_MODEL), BF16),
        compiler_params=_params(("arbitrary",)),
        name="input_norm",
    )(x2d, gain)


def _qkv_proj_kernel(xn_ref, wq_ref, wk_ref, wv_ref, cos_ref, sin_ref,
                     q1_ref, q4_ref, q16_ref, k1_ref, k4_ref, k16_ref,
                     v1_ref, v4_ref, v16_ref, qh_ref, kh_ref, vh_ref):
    tm = xn_ref.shape[0]
    xn = xn_ref[...]
    jobs = ((wq_ref, True, q1_ref, q4_ref, q16_ref, qh_ref),
            (wk_ref, True, k1_ref, k4_ref, k16_ref, kh_ref),
            (wv_ref, False, v1_ref, v4_ref, v16_ref, vh_ref))
    for w_ref, rotate, nat_ref, d4_ref, d16_ref, head_ref in jobs:
        z = jnp.dot(xn, w_ref[...], preferred_element_type=F32)
        for h in range(ATTN_HEADS):
            cols = slice(h * HEAD_DIM, (h + 1) * HEAD_DIM)
            y = z[:, cols]
            if rotate:
                y = y * cos_ref[...] + pltpu.roll(y, HEAD_DIM // 2, 1) * sin_ref[...]
            nat_ref[:, cols] = y.astype(BF16)
            head_ref[h] = y
            for d, dst in ((4, d4_ref), (16, d16_ref)):
                for r in range(d):
                    dst[r, :, cols] = head_ref[h, pl.ds(r, tm // d, stride=d), :].astype(BF16)


def _qkv_proj(xn, w_in_bf16, layer, cos_t, sin_t, batch, seq):
    m = xn.shape[0]
    tm = PROJ_TM
    nt = seq // tm

    nat_shape = jax.ShapeDtypeStruct((m, ATTN_WIDTH), BF16)
    nat_spec = pl.BlockSpec((tm, ATTN_WIDTH), lambda i: (i, 0))

    def dil_shape(d):
        return jax.ShapeDtypeStruct((batch, d, seq // d, ATTN_WIDTH), BF16)

    def dil_spec(d):
        return pl.BlockSpec((None, d, tm // d, ATTN_WIDTH), lambda i: (i // nt, 0, i % nt, 0))

    def w_spec(col_block):
        return _resident((None, D_MODEL, ATTN_WIDTH), lambda i: (layer, 0, col_block))

    table_spec = pl.BlockSpec((tm, HEAD_DIM), lambda i: (i % nt, 0))
    head_scratch = pltpu.VMEM((ATTN_HEADS, tm, HEAD_DIM), F32)
    return pl.pallas_call(
        _qkv_proj_kernel,
        grid=(m // tm,),
        in_specs=[pl.BlockSpec((tm, D_MODEL), lambda i: (i, 0)),
                  w_spec(0), w_spec(1), w_spec(2), table_spec, table_spec],
        out_specs=[nat_spec, dil_spec(4), dil_spec(16)] * 3,
        out_shape=[nat_shape, dil_shape(4), dil_shape(16)] * 3,
        scratch_shapes=[head_scratch] * 3,
        compiler_params=_params(("arbitrary",)),
        name="qkv_proj",
    )(xn, w_in_bf16, w_in_bf16, w_in_bf16, cos_t, sin_t)


def _gmlp_kernel(xn_ref, wu_ref, wv_ref, lng_ref, lnb_ref, ws_ref, bs_ref, out_ref):
    tm = xn_ref.shape[0]
    xn = xn_ref[...]
    a = jax.nn.gelu(jnp.dot(xn, wv_ref[...], preferred_element_type=F32))
    u = jax.nn.gelu(jnp.dot(xn, wu_ref[...], preferred_element_type=F32))
    row = lax.broadcasted_iota(jnp.int32, (CHUNK, CHUNK), 0)
    col = lax.broadcasted_iota(jnp.int32, (CHUNK, CHUNK), 1)
    causal = col <= row
    for g in range(GMLP_GROUPS):
        cols = slice(g * GMLP_GROUP_DIM, (g + 1) * GMLP_GROUP_DIM)
        ag = a[:, cols]
        mu = jnp.mean(ag, axis=-1, keepdims=True)
        xc = ag - mu
        var = jnp.mean(xc * xc, axis=-1, keepdims=True)
        vn = (xc * lax.rsqrt(var + LN_EPS) * lng_ref[:, cols] + lnb_ref[:, cols]).astype(BF16)
        w = jnp.where(causal, ws_ref[g], 0.0).astype(BF16)
        bias = bs_ref[:, g:g + 1]
        for c in range(tm // CHUNK):
            rows = slice(c * CHUNK, (c + 1) * CHUNK)
            sp = jnp.dot(w, vn[rows], preferred_element_type=F32) + bias
            out_ref[rows, cols] = (u[rows, cols] * sp).astype(out_ref.dtype)


def _gmlp(xn, w_in_bf16, layer, ln_g, ln_b, w_spatial, b_spatial_t):
    m = xn.shape[0]
    tm = GMLP_TM
    first_block = 3 * ATTN_WIDTH // GMLP_WIDTH
    vec_spec = pl.BlockSpec((1, GMLP_WIDTH), lambda i: (0, 0))
    return pl.pallas_call(
        _gmlp_kernel,
        grid=(m // tm,),
        in_specs=[pl.BlockSpec((tm, D_MODEL), lambda i: (i, 0)),
                  _resident((None, D_MODEL, GMLP_WIDTH), lambda i: (layer, 0, first_block)),
                  _resident((None, D_MODEL, GMLP_WIDTH), lambda i: (layer, 0, first_block + 1)),
                  vec_spec, vec_spec,
                  pl.BlockSpec((GMLP_GROUPS, CHUNK, CHUNK), lambda i: (0, 0, 0)),
                  pl.BlockSpec((CHUNK, GMLP_GROUPS), lambda i: (0, 0))],
        out_specs=pl.BlockSpec((tm, GMLP_WIDTH), lambda i: (i, 0)),
        out_shape=jax.ShapeDtypeStruct((m, GMLP_WIDTH), BF16),
        compiler_params=_params(("arbitrary",)),
        name="gmlp",
    )(xn, w_in_bf16, w_in_bf16, ln_g, ln_b, w_spatial, b_spatial_t)


def _attention_kernel(q1_ref, k1_ref, v1_ref, q4_ref, k4_ref, v4_ref,
                      q16_ref, k16_ref, v16_ref, out_ref, acc_ref, max_ref, sum_ref):
    seq = out_ref.shape[0]
    blk = ATTN_BLOCK
    exp2_scale = HEAD_DIM ** -0.5 * math.log2(math.e)
    row = lax.broadcasted_iota(jnp.int32, (blk, blk), 0)
    col = lax.broadcasted_iota(jnp.int32, (blk, blk), 1)
    cur_mask = col <= row
    prev_mask = col >= row
    nt_dims = (((1,), (1,)), ((), ()))
    ones = jnp.ones((2 * blk, HEAD_DIM), BF16)

    def attend(br, d, q_ref, k_ref, v_ref, idx):
        nb = seq // (d * blk)
        n = idx % nb
        r = idx // nb
        static = isinstance(idx, int)
        q = q_ref[pl.ds(idx * blk if static else pl.multiple_of(idx * blk, blk), blk), :]
        if static and idx == 0:
            keys = pl.ds(0, blk)
            s = lax.dot_general(q, k_ref[keys, :], nt_dims, preferred_element_type=F32)
            s = jnp.where(cur_mask, s, -jnp.inf)
            m = jnp.max(s, axis=-1, keepdims=True)
            p = jnp.exp2((s - m) * exp2_scale).astype(BF16)
            n_keys = blk
        else:
            first = (idx - 1) * blk
            keys = pl.ds(first if static else pl.multiple_of(first, blk), 2 * blk)
            s = lax.dot_general(q, k_ref[keys, :], nt_dims, preferred_element_type=F32)
            s_p = jnp.where(jnp.logical_and(prev_mask, n > 0), s[:, :blk], -jnp.inf)
            s_c = jnp.where(cur_mask, s[:, blk:], -jnp.inf)
            m = jnp.max(jnp.maximum(s_c, s_p), axis=-1, keepdims=True)
            p = jnp.concatenate([jnp.exp2((s_p - m) * exp2_scale).astype(BF16),
                                 jnp.exp2((s_c - m) * exp2_scale).astype(BF16)], axis=1)
            n_keys = 2 * blk
        v_ext = jnp.concatenate([v_ref[keys, :], ones[:n_keys]], axis=1)
        acc = jnp.dot(p, v_ext, preferred_element_type=F32)
        start = n * (blk * d) + r
        if d == 1:
            rows = pl.ds(start if static else pl.multiple_of(start, blk), blk)
        else:
            rows = pl.ds(start, blk, stride=d)
        acc_ref[br, rows, :] = acc[:, :HEAD_DIM]
        sum_ref[br, rows, :] = acc[:, HEAD_DIM:]
        max_ref[br, rows, :] = jnp.broadcast_to(m, (blk, HEAD_DIM))

    branches = ((1, q1_ref, k1_ref, v1_ref),
                (4, q4_ref, k4_ref, v4_ref),
                (16, q16_ref, k16_ref, v16_ref))
    n_groups = seq // (blk * ATTN_UNROLL)
    for br, branch in enumerate(branches):
        for u in range(ATTN_UNROLL):
            attend(br, *branch, u)

        def group(g, carry, br=br, branch=branch):
            for u in range(ATTN_UNROLL):
                attend(br, *branch, g * ATTN_UNROLL + u)
            return carry

        lax.fori_loop(1, n_groups, group, 0)

    def merge(c, carry):
        rows = pl.ds(pl.multiple_of(c * MERGE_ROWS, MERGE_ROWS), MERGE_ROWS)
        m0, m1, m2 = max_ref[0, rows, :], max_ref[1, rows, :], max_ref[2, rows, :]
        top = jnp.maximum(jnp.maximum(m0, m1), m2)
        e0 = jnp.exp2((m0 - top) * exp2_scale)
        e1 = jnp.exp2((m1 - top) * exp2_scale)
        e2 = jnp.exp2((m2 - top) * exp2_scale)
        den = e0 * sum_ref[0, rows, :] + e1 * sum_ref[1, rows, :] + e2 * sum_ref[2, rows, :]
        num = e0 * acc_ref[0, rows, :] + e1 * acc_ref[1, rows, :] + e2 * acc_ref[2, rows, :]
        out_ref[rows, :] = (num / den).astype(out_ref.dtype)
        return carry

    lax.fori_loop(0, seq // MERGE_ROWS, merge, 0)


def _attention(qkv, batch, seq):
    spec = pl.BlockSpec((None, seq, HEAD_DIM), lambda b, h: (b, 0, h))
    stat = pltpu.VMEM((len(DILATIONS), seq, HEAD_DIM), F32)
    return pl.pallas_call(
        _attention_kernel,
        grid=(batch, ATTN_HEADS),
        in_specs=[spec] * 9,
        out_specs=spec,
        out_shape=jax.ShapeDtypeStruct((batch, seq, ATTN_WIDTH), BF16),
        scratch_shapes=[stat, stat, stat],
        compiler_params=_params(("arbitrary", "arbitrary")),
        name="dilated_attention",
    )(*qkv)


def _out_proj_kernel(attn_ref, gm_ref, wa_ref, wg_ref, x_ref, g_ref, x1_ref, xn_ref):
    x1 = (x_ref[...]
          + jnp.dot(attn_ref[...], wa_ref[...], preferred_element_type=F32)
          + jnp.dot(gm_ref[...], wg_ref[...], preferred_element_type=F32))
    x1_ref[...] = x1
    xn_ref[...] = _rms_normalize(x1, g_ref[...]).astype(BF16)


def _out_proj(attn, gm, w_out_bf16, layer, x2d, gain):
    m = x2d.shape[0]
    tm = OUT_TM
    gm_row_block = ATTN_WIDTH // GMLP_WIDTH
    row_spec = pl.BlockSpec((tm, D_MODEL), lambda i: (i, 0))
    return pl.pallas_call(
        _out_proj_kernel,
        grid=(m // tm,),
        in_specs=[
            pl.BlockSpec((tm, ATTN_WIDTH), lambda i: (i, 0)),
            pl.BlockSpec((tm, GMLP_WIDTH), lambda i: (i, 0)),
            _resident((None, ATTN_WIDTH, D_MODEL), lambda i: (layer, 0, 0)),
            _resident((None, GMLP_WIDTH, D_MODEL), lambda i: (layer, gm_row_block, 0)),
            row_spec,
            pl.BlockSpec((1, D_MODEL), lambda i: (0, 0)),
        ],
        out_specs=[row_spec, row_spec],
        out_shape=[jax.ShapeDtypeStruct((m, D_MODEL), F32),
                   jax.ShapeDtypeStruct((m, D_MODEL), BF16)],
        compiler_params=_params(("arbitrary",)),
        name="out_proj",
    )(attn, gm, w_out_bf16, w_out_bf16, x2d, gain)


def _mlp_kernel(x_ref, xn_ref, wu_ref, wd_ref, g_ref, *out_refs, last_layer):
    f = pl.program_id(1)
    acc_ref = out_refs[0]

    @pl.when(f == 0)
    def _():
        acc_ref[...] = x_ref[...]

    h = jnp.dot(xn_ref[...], wu_ref[...], preferred_element_type=F32)
    h = jnp.square(jnp.maximum(h, 0.0)).astype(BF16)
    acc_ref[...] += jnp.dot(h, wd_ref[...], preferred_element_type=F32)

    @pl.when(f == pl.num_programs(1) - 1)
    def _():
        normed = _rms_normalize(acc_ref[...], g_ref[...])
        if last_layer:
            acc_ref[...] = normed
        else:
            out_refs[1][...] = normed.astype(BF16)


def _mlp(x1, xn, w_up_bf16, w_down_bf16, layer, next_gain, last_layer):
    m = x1.shape[0]
    tm, tf = MLP_TM, MLP_TF
    row_spec = pl.BlockSpec((tm, D_MODEL), lambda i, f: (i, 0))
    out_specs = [row_spec]
    out_shape = [jax.ShapeDtypeStruct((m, D_MODEL), F32)]
    if not last_layer:
        out_specs.append(row_spec)
        out_shape.append(jax.ShapeDtypeStruct((m, D_MODEL), BF16))
    return pl.pallas_call(
        functools.partial(_mlp_kernel, last_layer=last_layer),
        grid=(m // tm, D_FF // tf),
        in_specs=[
            row_spec,
            row_spec,
            pl.BlockSpec((None, D_MODEL, tf), lambda i, f: (layer, 0, f)),
            pl.BlockSpec((None, tf, D_MODEL), lambda i, f: (layer, f, 0)),
            pl.BlockSpec((1, D_MODEL), lambda i, f: (0, 0)),
        ],
        out_specs=out_specs,
        out_shape=out_shape,
        compiler_params=_params(("arbitrary", "arbitrary")),
        name="mlp",
    )(x1, xn, w_up_bf16, w_down_bf16, next_gain)


def _rotary_tables(seq):
    half = HEAD_DIM // 2
    inv_freq = ROPE_THETA ** (-jnp.arange(half, dtype=F32) / half)
    ang = jnp.arange(seq, dtype=jnp.int32).astype(F32)[:, None] * inv_freq[None, :]
    cos, sin = jnp.cos(ang), jnp.sin(ang)
    return jnp.concatenate([cos, cos], axis=-1), jnp.concatenate([-sin, sin], axis=-1)


def kernel(x, norm1_g, w_in, gmlp_ln_g, gmlp_ln_b, w_spatial, b_spatial, w_out,
           norm2_g, w_up, w_down, final_g):
    batch, seq, _ = x.shape
    depth = w_in.shape[0]
    m = batch * seq
    cos_t, sin_t = _rotary_tables(seq)
    gains1 = norm1_g.reshape(depth, 1, D_MODEL)
    gains2 = norm2_g.reshape(depth, 1, D_MODEL)
    final_gain = final_g.reshape(1, D_MODEL)

    w_in_bf16, w_out_bf16 = w_in.astype(BF16), w_out.astype(BF16)
    w_up_bf16, w_down_bf16 = w_up.astype(BF16), w_down.astype(BF16)

    x2d = x.reshape(m, D_MODEL)
    xn = _norm(x2d, gains1[0])
    for l in range(depth):
        last_layer = l == depth - 1
        outs = _qkv_proj(xn, w_in_bf16, l, cos_t, sin_t, batch, seq)
        qkv = [outs[3 * t + b].reshape(batch, seq, ATTN_WIDTH)
               for b in range(len(DILATIONS)) for t in range(3)]
        gm = _gmlp(xn, w_in_bf16, l, gmlp_ln_g[l].reshape(1, GMLP_WIDTH),
                   gmlp_ln_b[l].reshape(1, GMLP_WIDTH), w_spatial[l], jnp.transpose(b_spatial[l]))
        attn = _attention(qkv, batch, seq).reshape(m, ATTN_WIDTH)
        x1, xn2 = _out_proj(attn, gm, w_out_bf16, l, x2d, gains2[l])
        next_gain = final_gain if last_layer else gains1[l + 1]
        res = _mlp(x1, xn2, w_up_bf16, w_down_bf16, l, next_gain, last_layer)
        if last_layer:
            x2d = res[0]
        else:
            x2d, xn = res
    return x2d.reshape(batch, seq, D_MODEL)
```

```python
import functools
import math

import jax
import jax.numpy as jnp
from jax import lax
from jax.experimental import pallas as pl
from jax.experimental.pallas import tpu as pltpu

D_MODEL = 2048
HEAD_DIM = 128
ATTN_HEADS = 8
ATTN_WIDTH = ATTN_HEADS * HEAD_DIM
GMLP_GROUPS = 8
GMLP_GROUP_DIM = 128
GMLP_WIDTH = GMLP_GROUPS * GMLP_GROUP_DIM
CHUNK = 128
ATTN_BLOCK = 128
DILATIONS = (1, 4, 16)
ROPE_THETA = 10000.0
D_FF = 4 * D_MODEL
NORM_EPS = 1e-6
LN_EPS = 1e-5

V7X_VMEM_LIMIT_BYTES = 56 * 1024 * 1024

NORM_TM = 1024
PROJ_TM = 512
GMLP_TM = 1024
OUT_TM = 512
MLP_TM = 512
MLP_TF = 1024
MERGE_ROWS = 256
MAX_SINGLE_ACCESS_STRIDE = 4
STAGING_SLOT = len(DILATIONS)

F32 = jnp.float32
BF16 = jnp.bfloat16


def _rms_normalize(x, gain):
    return x * lax.rsqrt(jnp.mean(x * x, axis=-1, keepdims=True) + NORM_EPS) * gain


def _params(semantics):
    return pltpu.CompilerParams(dimension_semantics=semantics,
                                vmem_limit_bytes=V7X_VMEM_LIMIT_BYTES)


def _resident(shape, index_map):
    return pl.BlockSpec(shape, index_map, pipeline_mode=pl.Buffered(1))


def _norm_kernel(x_ref, g_ref, out_ref):
    out_ref[...] = _rms_normalize(x_ref[...], g_ref[...]).astype(out_ref.dtype)


def _norm(x2d, gain):
    m = x2d.shape[0]
    spec = pl.BlockSpec((NORM_TM, D_MODEL), lambda i: (i, 0))
    return pl.pallas_call(
        _norm_kernel,
        grid=(m // NORM_TM,),
        in_specs=[spec, pl.BlockSpec((1, D_MODEL), lambda i: (0, 0))],
        out_specs=spec,
        out_shape=jax.ShapeDtypeStruct((m, D_MODEL), BF16),
        compiler_params=_params(("arbitrary",)),
        name="input_norm",
    )(x2d, gain)


def _qkv_proj_kernel(xn_ref, wq_ref, wk_ref, wv_ref, cos_ref, sin_ref,
                     q1_ref, q4_ref, q16_ref, k1_ref, k4_ref, k16_ref,
                     v1_ref, v4_ref, v16_ref, qh_ref, kh_ref, vh_ref, q4h_ref, k4h_ref, v4h_ref):
    tm = xn_ref.shape[0]
    quarter = tm // 4
    xn = xn_ref[...]
    jobs = ((wq_ref, True, q1_ref, q4_ref, q16_ref, qh_ref, q4h_ref),
            (wk_ref, True, k1_ref, k4_ref, k16_ref, kh_ref, k4h_ref),
            (wv_ref, False, v1_ref, v4_ref, v16_ref, vh_ref, v4h_ref))
    for w_ref, rotate, nat_ref, d4_ref, d16_ref, head_ref, head4_ref in jobs:
        z = jnp.dot(xn, w_ref[...], preferred_element_type=F32)
        for h in range(ATTN_HEADS):
            cols = slice(h * HEAD_DIM, (h + 1) * HEAD_DIM)
            y = z[:, cols]
            if rotate:
                y = y * cos_ref[...] + pltpu.roll(y, HEAD_DIM // 2, 1) * sin_ref[...]
            nat_ref[:, cols] = y.astype(BF16)
            head_ref[h] = y
            for r4 in range(4):
                y4 = head_ref[h, pl.ds(r4, quarter, stride=4), :]
                d4_ref[r4, :, cols] = y4.astype(BF16)
                head4_ref[h, r4 * quarter:(r4 + 1) * quarter, :] = y4
            for r16 in range(16):
                c, r4 = divmod(r16, 4)
                y16 = head4_ref[h, pl.ds(r4 * quarter + c, tm // 16, stride=4), :]
                d16_ref[r16, :, cols] = y16.astype(BF16)


def _qkv_proj(xn, w_in_bf16, layer, cos_t, sin_t, batch, seq):
    m = xn.shape[0]
    tm = PROJ_TM
    nt = seq // tm

    nat_shape = jax.ShapeDtypeStruct((m, ATTN_WIDTH), BF16)
    nat_spec = pl.BlockSpec((tm, ATTN_WIDTH), lambda i: (i, 0))

    def dil_shape(d):
        return jax.ShapeDtypeStruct((batch, d, seq // d, ATTN_WIDTH), BF16)

    def dil_spec(d):
        return pl.BlockSpec((None, d, tm // d, ATTN_WIDTH), lambda i: (i // nt, 0, i % nt, 0))

    def w_spec(col_block):
        return _resident((None, D_MODEL, ATTN_WIDTH), lambda i: (layer, 0, col_block))

    table_spec = pl.BlockSpec((tm, HEAD_DIM), lambda i: (i % nt, 0))
    head_scratch = pltpu.VMEM((ATTN_HEADS, tm, HEAD_DIM), F32)
    return pl.pallas_call(
        _qkv_proj_kernel,
        grid=(m // tm,),
        in_specs=[pl.BlockSpec((tm, D_MODEL), lambda i: (i, 0)),
                  w_spec(0), w_spec(1), w_spec(2), table_spec, table_spec],
        out_specs=[nat_spec, dil_spec(4), dil_spec(16)] * 3,
        out_shape=[nat_shape, dil_shape(4), dil_shape(16)] * 3,
        scratch_shapes=[head_scratch] * 6,
        compiler_params=_params(("arbitrary",)),
        name="qkv_proj",
    )(xn, w_in_bf16, w_in_bf16, w_in_bf16, cos_t, sin_t)


def _gmlp_kernel(xn_ref, wu_ref, wv_ref, lng_ref, lnb_ref, ws_ref, bs_ref, out_ref):
    tm = xn_ref.shape[0]
    xn = xn_ref[...]
    a = jax.nn.gelu(jnp.dot(xn, wv_ref[...], preferred_element_type=F32))
    u = jax.nn.gelu(jnp.dot(xn, wu_ref[...], preferred_element_type=F32))
    row = lax.broadcasted_iota(jnp.int32, (CHUNK, CHUNK), 0)
    col = lax.broadcasted_iota(jnp.int32, (CHUNK, CHUNK), 1)
    causal = col <= row
    for g in range(GMLP_GROUPS):
        cols = slice(g * GMLP_GROUP_DIM, (g + 1) * GMLP_GROUP_DIM)
        ag = a[:, cols]
        mu = jnp.mean(ag, axis=-1, keepdims=True)
        xc = ag - mu
        var = jnp.mean(xc * xc, axis=-1, keepdims=True)
        vn = (xc * lax.rsqrt(var + LN_EPS) * lng_ref[:, cols] + lnb_ref[:, cols]).astype(BF16)
        w = jnp.where(causal, ws_ref[g], 0.0).astype(BF16)
        bias = bs_ref[:, g:g + 1]
        for c in range(tm // CHUNK):
            rows = slice(c * CHUNK, (c + 1) * CHUNK)
            sp = jnp.dot(w, vn[rows], preferred_element_type=F32) + bias
            out_ref[rows, cols] = (u[rows, cols] * sp).astype(out_ref.dtype)


def _gmlp(xn, w_in_bf16, layer, ln_g, ln_b, w_spatial, b_spatial_t):
    m = xn.shape[0]
    tm = GMLP_TM
    first_block = 3 * ATTN_WIDTH // GMLP_WIDTH
    vec_spec = pl.BlockSpec((1, GMLP_WIDTH), lambda i: (0, 0))
    return pl.pallas_call(
        _gmlp_kernel,
        grid=(m // tm,),
        in_specs=[pl.BlockSpec((tm, D_MODEL), lambda i: (i, 0)),
                  _resident((None, D_MODEL, GMLP_WIDTH), lambda i: (layer, 0, first_block)),
                  _resident((None, D_MODEL, GMLP_WIDTH), lambda i: (layer, 0, first_block + 1)),
                  vec_spec, vec_spec,
                  pl.BlockSpec((GMLP_GROUPS, CHUNK, CHUNK), lambda i: (0, 0, 0)),
                  pl.BlockSpec((CHUNK, GMLP_GROUPS), lambda i: (0, 0))],
        out_specs=pl.BlockSpec((tm, GMLP_WIDTH), lambda i: (i, 0)),
        out_shape=jax.ShapeDtypeStruct((m, GMLP_WIDTH), BF16),
        compiler_params=_params(("arbitrary",)),
        name="gmlp",
    )(xn, w_in_bf16, w_in_bf16, ln_g, ln_b, w_spatial, b_spatial_t)


def _attention_kernel(q1_ref, k1_ref, v1_ref, q4_ref, k4_ref, v4_ref,
                      q16_ref, k16_ref, v16_ref, out_ref, acc_ref, max_ref, sum_ref):
    seq = out_ref.shape[0]
    blk = ATTN_BLOCK
    exp2_scale = HEAD_DIM ** -0.5 * math.log2(math.e)
    row = lax.broadcasted_iota(jnp.int32, (blk, blk), 0)
    col = lax.broadcasted_iota(jnp.int32, (blk, blk), 1)
    cur_mask = col <= row
    prev_mask = col >= row
    nt_dims = (((1,), (1,)), ((), ()))
    ones = jnp.ones((2 * blk, HEAD_DIM), BF16)

    def attend(br, d, q_ref, k_ref, v_ref, idx):
        nb = seq // (d * blk)
        n, r = idx % nb, idx // nb
        q = q_ref[pl.ds(idx * blk, blk), :]
        if n == 0:
            n_keys = blk
            keys = pl.ds(idx * blk, n_keys)
            s = lax.dot_general(q, k_ref[keys, :], nt_dims, preferred_element_type=F32)
            s = jnp.where(cur_mask, s, -jnp.inf)
            m = jnp.max(s, axis=-1, keepdims=True)
            p = jnp.exp2((s - m) * exp2_scale).astype(BF16)
        else:
            n_keys = 2 * blk
            keys = pl.ds((idx - 1) * blk, n_keys)
            s = lax.dot_general(q, k_ref[keys, :], nt_dims, preferred_element_type=F32)
            s_p = jnp.where(prev_mask, s[:, :blk], -jnp.inf)
            s_c = jnp.where(cur_mask, s[:, blk:], -jnp.inf)
            m = jnp.max(jnp.maximum(s_c, s_p), axis=-1, keepdims=True)
            p = jnp.concatenate([jnp.exp2((s_p - m) * exp2_scale).astype(BF16),
                                 jnp.exp2((s_c - m) * exp2_scale).astype(BF16)], axis=1)
        v_ext = jnp.concatenate([v_ref[keys, :], ones[:n_keys]], axis=1)
        acc = jnp.dot(p, v_ext, preferred_element_type=F32)
        if d <= MAX_SINGLE_ACCESS_STRIDE:
            dst = br
            rows = pl.ds(n * (blk * d) + r, blk, stride=d) if d > 1 else pl.ds(n * blk, blk)
        else:
            dst = STAGING_SLOT
            rows = pl.ds((r % 4) * (seq // 4) + (n * blk * d + r) // 4, blk, stride=d // 4)
        acc_ref[dst, rows, :] = acc[:, :HEAD_DIM]
        sum_ref[dst, rows, :] = acc[:, HEAD_DIM:]
        max_ref[dst, rows, :] = jnp.broadcast_to(m, (blk, HEAD_DIM))

    branches = ((1, q1_ref, k1_ref, v1_ref),
                (4, q4_ref, k4_ref, v4_ref),
                (16, q16_ref, k16_ref, v16_ref))
    for br, (d, q_ref, k_ref, v_ref) in enumerate(branches):
        for idx in range(seq // blk):
            attend(br, d, q_ref, k_ref, v_ref, idx)
        if d > MAX_SINGLE_ACCESS_STRIDE:
            for ref in (acc_ref, sum_ref, max_ref):
                for r4 in range(4):
                    for part in range(seq // (4 * MERGE_ROWS)):
                        u0 = part * MERGE_ROWS
                        src = pl.ds(r4 * (seq // 4) + u0, MERGE_ROWS)
                        dst_rows = pl.ds(4 * u0 + r4, MERGE_ROWS, stride=4)
                        ref[br, dst_rows, :] = ref[STAGING_SLOT, src, :]

    def merge(c, carry):
        rows = pl.ds(pl.multiple_of(c * MERGE_ROWS, MERGE_ROWS), MERGE_ROWS)
        m0, m1, m2 = max_ref[0, rows, :], max_ref[1, rows, :], max_ref[2, rows, :]
        top = jnp.maximum(jnp.maximum(m0, m1), m2)
        e0 = jnp.exp2((m0 - top) * exp2_scale)
        e1 = jnp.exp2((m1 - top) * exp2_scale)
        e2 = jnp.exp2((m2 - top) * exp2_scale)
        den = e0 * sum_ref[0, rows, :] + e1 * sum_ref[1, rows, :] + e2 * sum_ref[2, rows, :]
        num = e0 * acc_ref[0, rows, :] + e1 * acc_ref[1, rows, :] + e2 * acc_ref[2, rows, :]
        out_ref[rows, :] = (num / den).astype(out_ref.dtype)
        return carry

    lax.fori_loop(0, seq // MERGE_ROWS, merge, 0)


def _attention(qkv, batch, seq):
    spec = pl.BlockSpec((None, seq, HEAD_DIM), lambda b, h: (b, 0, h))
    stat = pltpu.VMEM((len(DILATIONS) + 1, seq, HEAD_DIM), F32)
    return pl.pallas_call(
        _attention_kernel,
        grid=(batch, ATTN_HEADS),
        in_specs=[spec] * 9,
        out_specs=spec,
        out_shape=jax.ShapeDtypeStruct((batch, seq, ATTN_WIDTH), BF16),
        scratch_shapes=[stat, stat, stat],
        compiler_params=_params(("arbitrary", "arbitrary")),
        name="dilated_attention",
    )(*qkv)


def _out_proj_kernel(attn_ref, gm_ref, wa_ref, wg_ref, x_ref, g_ref, x1_ref, xn_ref):
    x1 = (x_ref[...]
          + jnp.dot(attn_ref[...], wa_ref[...], preferred_element_type=F32)
          + jnp.dot(gm_ref[...], wg_ref[...], preferred_element_type=F32))
    x1_ref[...] = x1
    xn_ref[...] = _rms_normalize(x1, g_ref[...]).astype(BF16)


def _out_proj(attn, gm, w_out_bf16, layer, x2d, gain):
    m = x2d.shape[0]
    tm = OUT_TM
    gm_row_block = ATTN_WIDTH // GMLP_WIDTH
    row_spec = pl.BlockSpec((tm, D_MODEL), lambda i: (i, 0))
    return pl.pallas_call(
        _out_proj_kernel,
        grid=(m // tm,),
        in_specs=[
            pl.BlockSpec((tm, ATTN_WIDTH), lambda i: (i, 0)),
            pl.BlockSpec((tm, GMLP_WIDTH), lambda i: (i, 0)),
            _resident((None, ATTN_WIDTH, D_MODEL), lambda i: (layer, 0, 0)),
            _resident((None, GMLP_WIDTH, D_MODEL), lambda i: (layer, gm_row_block, 0)),
            row_spec,
            pl.BlockSpec((1, D_MODEL), lambda i: (0, 0)),
        ],
        out_specs=[row_spec, row_spec],
        out_shape=[jax.ShapeDtypeStruct((m, D_MODEL), F32),
                   jax.ShapeDtypeStruct((m, D_MODEL), BF16)],
        compiler_params=_params(("arbitrary",)),
        name="out_proj",
    )(attn, gm, w_out_bf16, w_out_bf16, x2d, gain)


def _mlp_kernel(x_ref, xn_ref, wu_ref, wd_ref, g_ref, *out_refs, last_layer):
    f = pl.program_id(1)
    acc_ref = out_refs[0]

    @pl.when(f == 0)
    def _():
        acc_ref[...] = x_ref[...]

    h = jnp.dot(xn_ref[...], wu_ref[...], preferred_element_type=F32)
    h = jnp.square(jnp.maximum(h, 0.0)).astype(BF16)
    acc_ref[...] += jnp.dot(h, wd_ref[...], preferred_element_type=F32)

    @pl.when(f == pl.num_programs(1) - 1)
    def _():
        normed = _rms_normalize(acc_ref[...], g_ref[...])
        if last_layer:
            acc_ref[...] = normed
        else:
            out_refs[1][...] = normed.astype(BF16)


def _mlp(x1, xn, w_up_bf16, w_down_bf16, layer, next_gain, last_layer):
    m = x1.shape[0]
    tm, tf = MLP_TM, MLP_TF
    row_spec = pl.BlockSpec((tm, D_MODEL), lambda i, f: (i, 0))
    out_specs = [row_spec]
    out_shape = [jax.ShapeDtypeStruct((m, D_MODEL), F32)]
    if not last_layer:
        out_specs.append(row_spec)
        out_shape.append(jax.ShapeDtypeStruct((m, D_MODEL), BF16))
    return pl.pallas_call(
        functools.partial(_mlp_kernel, last_layer=last_layer),
        grid=(m // tm, D_FF // tf),
        in_specs=[
            row_spec,
            row_spec,
            pl.BlockSpec((None, D_MODEL, tf), lambda i, f: (layer, 0, f)),
            pl.BlockSpec((None, tf, D_MODEL), lambda i, f: (layer, f, 0)),
            pl.BlockSpec((1, D_MODEL), lambda i, f: (0, 0)),
        ],
        out_specs=out_specs,
        out_shape=out_shape,
        compiler_params=_params(("arbitrary", "arbitrary")),
        name="mlp",
    )(x1, xn, w_up_bf16, w_down_bf16, next_gain)


def _rotary_tables(seq):
    half = HEAD_DIM // 2
    inv_freq = ROPE_THETA ** (-jnp.arange(half, dtype=F32) / half)
    ang = jnp.arange(seq, dtype=jnp.int32).astype(F32)[:, None] * inv_freq[None, :]
    cos, sin = jnp.cos(ang), jnp.sin(ang)
    return jnp.concatenate([cos, cos], axis=-1), jnp.concatenate([-sin, sin], axis=-1)


def kernel(x, norm1_g, w_in, gmlp_ln_g, gmlp_ln_b, w_spatial, b_spatial, w_out,
           norm2_g, w_up, w_down, final_g):
    batch, seq, _ = x.shape
    depth = w_in.shape[0]
    m = batch * seq
    cos_t, sin_t = _rotary_tables(seq)
    gains1 = norm1_g.reshape(depth, 1, D_MODEL)
    gains2 = norm2_g.reshape(depth, 1, D_MODEL)
    final_gain = final_g.reshape(1, D_MODEL)

    w_in_bf16, w_out_bf16 = w_in.astype(BF16), w_out.astype(BF16)
    w_up_bf16, w_down_bf16 = w_up.astype(BF16), w_down.astype(BF16)

    x2d = x.reshape(m, D_MODEL)
    xn = _norm(x2d, gains1[0])
    for l in range(depth):
        last_layer = l == depth - 1
        outs = _qkv_proj(xn, w_in_bf16, l, cos_t, sin_t, batch, seq)
        qkv = [outs[3 * t + b].reshape(batch, seq, ATTN_WIDTH)
               for b in range(len(DILATIONS)) for t in range(3)]
        gm = _gmlp(xn, w_in_bf16, l, gmlp_ln_g[l].reshape(1, GMLP_WIDTH),
                   gmlp_ln_b[l].reshape(1, GMLP_WIDTH), w_spatial[l], jnp.transpose(b_spatial[l]))
        attn = _attention(qkv, batch, seq).reshape(m, ATTN_WIDTH)
        x1, xn2 = _out_proj(attn, gm, w_out_bf16, l, x2d, gains2[l])
        next_gain = final_gain if last_layer else gains1[l + 1]
        res = _mlp(x1, xn2, w_up_bf16, w_down_bf16, l, next_gain, last_layer)
        if last_layer:
            x2d = res[0]
        else:
            x2d, xn = res
    return x2d.reshape(batch, seq, D_MODEL)
```

```python
import functools
import math

import jax
import jax.numpy as jnp
from jax import lax
from jax.experimental import pallas as pl
from jax.experimental.pallas import tpu as pltpu

D_MODEL = 2048
HEAD_DIM = 128
ATTN_HEADS = 8
ATTN_WIDTH = ATTN_HEADS * HEAD_DIM
GMLP_GROUPS = 8
GMLP_GROUP_DIM = 128
GMLP_WIDTH = GMLP_GROUPS * GMLP_GROUP_DIM
CHUNK = 128
ATTN_BLOCK = 128
DILATIONS = (1, 4, 16)
ROPE_THETA = 10000.0
D_FF = 4 * D_MODEL
NORM_EPS = 1e-6
LN_EPS = 1e-5

V7X_VMEM_LIMIT_BYTES = 56 * 1024 * 1024

PROJ_TM = 512
GMLP_TM = 1024
OUT_TM = 512
MLP_TM = 512
MLP_TF = 1024
MERGE_ROWS = 256
MAX_SINGLE_ACCESS_STRIDE = 4
STAGING_SLOT = len(DILATIONS)

F32 = jnp.float32
BF16 = jnp.bfloat16


def _rms_normalize(x, gain):
    return x * lax.rsqrt(jnp.mean(x * x, axis=-1, keepdims=True) + NORM_EPS) * gain


def _params(semantics):
    return pltpu.CompilerParams(dimension_semantics=semantics,
                                vmem_limit_bytes=V7X_VMEM_LIMIT_BYTES)


def _resident(shape, index_map):
    return pl.BlockSpec(shape, index_map, pipeline_mode=pl.Buffered(1))


def _qkv_proj_kernel(xn_ref, wq_ref, wk_ref, wv_ref, cos_ref, sin_ref,
                     nat_ref, d4_ref, d16_ref, *stage_refs):
    tm = xn_ref.shape[0]
    quarter = tm // 4
    xn = xn_ref[...]
    jobs = ((wq_ref, True), (wk_ref, True), (wv_ref, False))
    for t, (w_ref, rotate) in enumerate(jobs):
        head_ref, head4_ref = stage_refs[2 * t], stage_refs[2 * t + 1]
        z = jnp.dot(xn, w_ref[...], preferred_element_type=F32)
        for h in range(ATTN_HEADS):
            y = z[:, h * HEAD_DIM:(h + 1) * HEAD_DIM]
            if rotate:
                y = y * cos_ref[...] + pltpu.roll(y, HEAD_DIM // 2, 1) * sin_ref[...]
            cols = slice(t * ATTN_WIDTH + h * HEAD_DIM, t * ATTN_WIDTH + (h + 1) * HEAD_DIM)
            nat_ref[:, cols] = y.astype(BF16)
            head_ref[h] = y
            for r4 in range(4):
                y4 = head_ref[h, pl.ds(r4, quarter, stride=4), :]
                d4_ref[r4, :, cols] = y4.astype(BF16)
                head4_ref[h, r4 * quarter:(r4 + 1) * quarter, :] = y4
            for r16 in range(16):
                c, r4 = divmod(r16, 4)
                y16 = head4_ref[h, pl.ds(r4 * quarter + c, tm // 16, stride=4), :]
                d16_ref[r16, :, cols] = y16.astype(BF16)


def _qkv_proj(xn, w_in_bf16, layer, cos_t, sin_t, batch, seq):
    m = xn.shape[0]
    tm = PROJ_TM
    nt = seq // tm
    width = 3 * ATTN_WIDTH

    def dil_shape(d):
        return jax.ShapeDtypeStruct((batch, d, seq // d, width), BF16)

    def dil_spec(d):
        return pl.BlockSpec((None, d, tm // d, width), lambda i: (i // nt, 0, i % nt, 0))

    def w_spec(col_block):
        return _resident((None, D_MODEL, ATTN_WIDTH), lambda i: (layer, 0, col_block))

    table_spec = pl.BlockSpec((tm, HEAD_DIM), lambda i: (i % nt, 0))
    head_scratch = pltpu.VMEM((ATTN_HEADS, tm, HEAD_DIM), F32)
    return pl.pallas_call(
        _qkv_proj_kernel,
        grid=(m // tm,),
        in_specs=[pl.BlockSpec((tm, D_MODEL), lambda i: (i, 0)),
                  w_spec(0), w_spec(1), w_spec(2), table_spec, table_spec],
        out_specs=[pl.BlockSpec((tm, width), lambda i: (i, 0)), dil_spec(4), dil_spec(16)],
        out_shape=[jax.ShapeDtypeStruct((m, width), BF16), dil_shape(4), dil_shape(16)],
        scratch_shapes=[head_scratch] * 6,
        compiler_params=_params(("arbitrary",)),
        name="qkv_proj",
    )(xn, w_in_bf16, w_in_bf16, w_in_bf16, cos_t, sin_t)


def _gmlp_body(xn, wu_ref, wv_ref, lng_ref, lnb_ref, ws_ref, bs_ref, out_ref):
    tm = xn.shape[0]
    a = jax.nn.gelu(jnp.dot(xn, wv_ref[...], preferred_element_type=F32))
    u = jax.nn.gelu(jnp.dot(xn, wu_ref[...], preferred_element_type=F32))
    row = lax.broadcasted_iota(jnp.int32, (CHUNK, CHUNK), 0)
    col = lax.broadcasted_iota(jnp.int32, (CHUNK, CHUNK), 1)
    causal = col <= row
    for g in range(GMLP_GROUPS):
        cols = slice(g * GMLP_GROUP_DIM, (g + 1) * GMLP_GROUP_DIM)
        ag = a[:, cols]
        mu = jnp.mean(ag, axis=-1, keepdims=True)
        xc = ag - mu
        var = jnp.mean(xc * xc, axis=-1, keepdims=True)
        vn = (xc * lax.rsqrt(var + LN_EPS) * lng_ref[:, cols] + lnb_ref[:, cols]).astype(BF16)
        w = jnp.where(causal, ws_ref[g], 0.0).astype(BF16)
        bias = bs_ref[:, g:g + 1]
        for c in range(tm // CHUNK):
            rows = slice(c * CHUNK, (c + 1) * CHUNK)
            sp = jnp.dot(w, vn[rows], preferred_element_type=F32) + bias
            out_ref[rows, cols] = (u[rows, cols] * sp).astype(out_ref.dtype)


def _gmlp_kernel(xn_ref, *refs):
    _gmlp_body(xn_ref[...], *refs)


def _gmlp_from_input_kernel(x_ref, g_ref, *refs):
    *body_refs, xn_ref = refs
    xn = _rms_normalize(x_ref[...], g_ref[...]).astype(BF16)
    xn_ref[...] = xn
    _gmlp_body(xn, *body_refs)


def _gmlp(x_or_xn, gain, w_in_bf16, layer, ln_g, ln_b, w_spatial, b_spatial_t):
    m = x_or_xn.shape[0]
    tm = GMLP_TM
    first_block = 3 * ATTN_WIDTH // GMLP_WIDTH
    row_spec = pl.BlockSpec((tm, D_MODEL), lambda i: (i, 0))
    vec_spec = pl.BlockSpec((1, GMLP_WIDTH), lambda i: (0, 0))
    in_specs = [_resident((None, D_MODEL, GMLP_WIDTH), lambda i: (layer, 0, first_block)),
                _resident((None, D_MODEL, GMLP_WIDTH), lambda i: (layer, 0, first_block + 1)),
                vec_spec, vec_spec,
                pl.BlockSpec((GMLP_GROUPS, CHUNK, CHUNK), lambda i: (0, 0, 0)),
                pl.BlockSpec((CHUNK, GMLP_GROUPS), lambda i: (0, 0))]
    operands = (w_in_bf16, w_in_bf16, ln_g, ln_b, w_spatial, b_spatial_t)
    out_specs = [pl.BlockSpec((tm, GMLP_WIDTH), lambda i: (i, 0))]
    out_shape = [jax.ShapeDtypeStruct((m, GMLP_WIDTH), BF16)]
    if gain is None:
        body, in_specs, operands = _gmlp_kernel, [row_spec] + in_specs, (x_or_xn,) + operands
    else:
        body = _gmlp_from_input_kernel
        in_specs = [row_spec, pl.BlockSpec((1, D_MODEL), lambda i: (0, 0))] + in_specs
        operands = (x_or_xn, gain) + operands
        out_specs.append(row_spec)
        out_shape.append(jax.ShapeDtypeStruct((m, D_MODEL), BF16))
    return pl.pallas_call(
        body,
        grid=(m // tm,),
        in_specs=in_specs,
        out_specs=out_specs,
        out_shape=out_shape,
        compiler_params=_params(("arbitrary",)),
        name="gmlp",
    )(*operands)


def _attention_kernel(q1_ref, k1_ref, v1_ref, q4_ref, k4_ref, v4_ref,
                      q16_ref, k16_ref, v16_ref, out_ref, acc_ref, max_ref, sum_ref):
    seq = out_ref.shape[0]
    blk = ATTN_BLOCK
    exp2_scale = HEAD_DIM ** -0.5 * math.log2(math.e)
    row = lax.broadcasted_iota(jnp.int32, (blk, blk), 0)
    col = lax.broadcasted_iota(jnp.int32, (blk, blk), 1)
    cur_mask = col <= row
    prev_mask = col >= row
    nt_dims = (((1,), (1,)), ((), ()))
    ones = jnp.ones((2 * blk, HEAD_DIM), BF16)

    def attend(br, d, q_ref, k_ref, v_ref, idx):
        nb = seq // (d * blk)
        n, r = idx % nb, idx // nb
        q = q_ref[pl.ds(idx * blk, blk), :]
        if n == 0:
            n_keys = blk
            keys = pl.ds(idx * blk, n_keys)
            s = lax.dot_general(q, k_ref[keys, :], nt_dims, preferred_element_type=F32)
            s = jnp.where(cur_mask, s, -jnp.inf)
            m = jnp.max(s, axis=-1, keepdims=True)
            p = jnp.exp2((s - m) * exp2_scale).astype(BF16)
        else:
            n_keys = 2 * blk
            keys = pl.ds((idx - 1) * blk, n_keys)
            s = lax.dot_general(q, k_ref[keys, :], nt_dims, preferred_element_type=F32)
            s_p = jnp.where(prev_mask, s[:, :blk], -jnp.inf)
            s_c = jnp.where(cur_mask, s[:, blk:], -jnp.inf)
            m = jnp.max(jnp.maximum(s_c, s_p), axis=-1, keepdims=True)
            p = jnp.concatenate([jnp.exp2((s_p - m) * exp2_scale).astype(BF16),
                                 jnp.exp2((s_c - m) * exp2_scale).astype(BF16)], axis=1)
        v_ext = jnp.concatenate([v_ref[keys, :], ones[:n_keys]], axis=1)
        acc = jnp.dot(p, v_ext, preferred_element_type=F32)
        if d <= MAX_SINGLE_ACCESS_STRIDE:
            dst = br
            rows = pl.ds(n * (blk * d) + r, blk, stride=d) if d > 1 else pl.ds(n * blk, blk)
        else:
            dst = STAGING_SLOT
            rows = pl.ds((r % 4) * (seq // 4) + (n * blk * d + r) // 4, blk, stride=d // 4)
        acc_ref[dst, rows, :] = acc[:, :HEAD_DIM]
        sum_ref[dst, rows, :] = acc[:, HEAD_DIM:]
        max_ref[dst, rows, :] = jnp.broadcast_to(m, (blk, HEAD_DIM))

    branches = ((1, q1_ref, k1_ref, v1_ref),
                (4, q4_ref, k4_ref, v4_ref),
                (16, q16_ref, k16_ref, v16_ref))
    for br, (d, q_ref, k_ref, v_ref) in enumerate(branches):
        for idx in range(seq // blk):
            attend(br, d, q_ref, k_ref, v_ref, idx)
        if d > MAX_SINGLE_ACCESS_STRIDE:
            for ref in (acc_ref, sum_ref, max_ref):
                for r4 in range(4):
                    for part in range(seq // (4 * MERGE_ROWS)):
                        u0 = part * MERGE_ROWS
                        src = pl.ds(r4 * (seq // 4) + u0, MERGE_ROWS)
                        dst_rows = pl.ds(4 * u0 + r4, MERGE_ROWS, stride=4)
                        ref[br, dst_rows, :] = ref[STAGING_SLOT, src, :]

    def merge(c, carry):
        rows = pl.ds(pl.multiple_of(c * MERGE_ROWS, MERGE_ROWS), MERGE_ROWS)
        m0, m1, m2 = max_ref[0, rows, :], max_ref[1, rows, :], max_ref[2, rows, :]
        top = jnp.maximum(jnp.maximum(m0, m1), m2)
        e0 = jnp.exp2((m0 - top) * exp2_scale)
        e1 = jnp.exp2((m1 - top) * exp2_scale)
        e2 = jnp.exp2((m2 - top) * exp2_scale)
        den = e0 * sum_ref[0, rows, :] + e1 * sum_ref[1, rows, :] + e2 * sum_ref[2, rows, :]
        num = e0 * acc_ref[0, rows, :] + e1 * acc_ref[1, rows, :] + e2 * acc_ref[2, rows, :]
        out_ref[rows, :] = (num / den).astype(out_ref.dtype)
        return carry

    lax.fori_loop(0, seq // MERGE_ROWS, merge, 0)


def _attention(qkv_by_order, batch, seq):
    def spec(t):
        return pl.BlockSpec((None, seq, HEAD_DIM), lambda b, h: (b, 0, t * ATTN_HEADS + h))

    out_spec = pl.BlockSpec((None, seq, HEAD_DIM), lambda b, h: (b, 0, h))
    stat = pltpu.VMEM((len(DILATIONS) + 1, seq, HEAD_DIM), F32)
    return pl.pallas_call(
        _attention_kernel,
        grid=(batch, ATTN_HEADS),
        in_specs=[spec(0), spec(1), spec(2)] * len(DILATIONS),
        out_specs=out_spec,
        out_shape=jax.ShapeDtypeStruct((batch, seq, ATTN_WIDTH), BF16),
        scratch_shapes=[stat, stat, stat],
        compiler_params=_params(("arbitrary", "arbitrary")),
        name="dilated_attention",
    )(*[a for a in qkv_by_order for _ in range(3)])


def _out_proj_kernel(attn_ref, gm_ref, wa_ref, wg_ref, x_ref, g_ref, x1_ref, xn_ref):
    x1 = (x_ref[...]
          + jnp.dot(attn_ref[...], wa_ref[...], preferred_element_type=F32)
          + jnp.dot(gm_ref[...], wg_ref[...], preferred_element_type=F32))
    x1_ref[...] = x1
    xn_ref[...] = _rms_normalize(x1, g_ref[...]).astype(BF16)


def _out_proj(attn, gm, w_out_bf16, layer, x2d, gain):
    m = x2d.shape[0]
    tm = OUT_TM
    gm_row_block = ATTN_WIDTH // GMLP_WIDTH
    row_spec = pl.BlockSpec((tm, D_MODEL), lambda i: (i, 0))
    return pl.pallas_call(
        _out_proj_kernel,
        grid=(m // tm,),
        in_specs=[
            pl.BlockSpec((tm, ATTN_WIDTH), lambda i: (i, 0)),
            pl.BlockSpec((tm, GMLP_WIDTH), lambda i: (i, 0)),
            _resident((None, ATTN_WIDTH, D_MODEL), lambda i: (layer, 0, 0)),
            _resident((None, GMLP_WIDTH, D_MODEL), lambda i: (layer, gm_row_block, 0)),
            row_spec,
            pl.BlockSpec((1, D_MODEL), lambda i: (0, 0)),
        ],
        out_specs=[row_spec, row_spec],
        out_shape=[jax.ShapeDtypeStruct((m, D_MODEL), F32),
                   jax.ShapeDtypeStruct((m, D_MODEL), BF16)],
        compiler_params=_params(("arbitrary",)),
        name="out_proj",
    )(attn, gm, w_out_bf16, w_out_bf16, x2d, gain)


def _mlp_kernel(x_ref, xn_ref, wu_ref, wd_ref, g_ref, *out_refs, last_layer):
    f = pl.program_id(1)
    acc_ref = out_refs[0]

    @pl.when(f == 0)
    def _():
        acc_ref[...] = x_ref[...]

    h = jnp.dot(xn_ref[...], wu_ref[...], preferred_element_type=F32)
    h = jnp.square(jnp.maximum(h, 0.0)).astype(BF16)
    acc_ref[...] += jnp.dot(h, wd_ref[...], preferred_element_type=F32)

    @pl.when(f == pl.num_programs(1) - 1)
    def _():
        normed = _rms_normalize(acc_ref[...], g_ref[...])
        if last_layer:
            acc_ref[...] = normed
        else:
            out_refs[1][...] = normed.astype(BF16)


def _mlp(x1, xn, w_up_bf16, w_down_bf16, layer, next_gain, last_layer):
    m = x1.shape[0]
    tm, tf = MLP_TM, MLP_TF
    row_spec = pl.BlockSpec((tm, D_MODEL), lambda i, f: (i, 0))
    out_specs = [row_spec]
    out_shape = [jax.ShapeDtypeStruct((m, D_MODEL), F32)]
    if not last_layer:
        out_specs.append(row_spec)
        out_shape.append(jax.ShapeDtypeStruct((m, D_MODEL), BF16))
    return pl.pallas_call(
        functools.partial(_mlp_kernel, last_layer=last_layer),
        grid=(m // tm, D_FF // tf),
        in_specs=[
            row_spec,
            row_spec,
            pl.BlockSpec((None, D_MODEL, tf), lambda i, f: (layer, 0, f)),
            pl.BlockSpec((None, tf, D_MODEL), lambda i, f: (layer, f, 0)),
            pl.BlockSpec((1, D_MODEL), lambda i, f: (0, 0)),
        ],
        out_specs=out_specs,
        out_shape=out_shape,
        compiler_params=_params(("arbitrary", "arbitrary")),
        name="mlp",
    )(x1, xn, w_up_bf16, w_down_bf16, next_gain)


def _rotary_tables(seq):
    half = HEAD_DIM // 2
    inv_freq = ROPE_THETA ** (-jnp.arange(half, dtype=F32) / half)
    ang = jnp.arange(seq, dtype=jnp.int32).astype(F32)[:, None] * inv_freq[None, :]
    cos, sin = jnp.cos(ang), jnp.sin(ang)
    return jnp.concatenate([cos, cos], axis=-1), jnp.concatenate([-sin, sin], axis=-1)


def kernel(x, norm1_g, w_in, gmlp_ln_g, gmlp_ln_b, w_spatial, b_spatial, w_out,
           norm2_g, w_up, w_down, final_g):
    batch, seq, _ = x.shape
    depth = w_in.shape[0]
    m = batch * seq
    cos_t, sin_t = _rotary_tables(seq)
    gains1 = norm1_g.reshape(depth, 1, D_MODEL)
    gains2 = norm2_g.reshape(depth, 1, D_MODEL)
    final_gain = final_g.reshape(1, D_MODEL)

    w_in_bf16, w_out_bf16 = w_in.astype(BF16), w_out.astype(BF16)
    w_up_bf16, w_down_bf16 = w_up.astype(BF16), w_down.astype(BF16)

    x2d = x.reshape(m, D_MODEL)
    xn = None
    for l in range(depth):
        last_layer = l == depth - 1
        gmlp_args = (w_in_bf16, l, gmlp_ln_g[l].reshape(1, GMLP_WIDTH),
                     gmlp_ln_b[l].reshape(1, GMLP_WIDTH), w_spatial[l], jnp.transpose(b_spatial[l]))
        if xn is None:
            gm, xn = _gmlp(x2d, gains1[l], *gmlp_args)
        else:
            gm, = _gmlp(xn, None, *gmlp_args)
        qkv = [a.reshape(batch, seq, 3 * ATTN_WIDTH)
               for a in _qkv_proj(xn, w_in_bf16, l, cos_t, sin_t, batch, seq)]
        attn = _attention(qkv, batch, seq).reshape(m, ATTN_WIDTH)
        x1, xn2 = _out_proj(attn, gm, w_out_bf16, l, x2d, gains2[l])
        next_gain = final_gain if last_layer else gains1[l + 1]
        res = _mlp(x1, xn2, w_up_bf16, w_down_bf16, l, next_gain, last_layer)
        if last_layer:
            x2d = res[0]
        else:
            x2d, xn = res
    return x2d.reshape(batch, seq, D_MODEL)
```

```python
import functools
import math

import jax
import jax.numpy as jnp
from jax import lax
from jax.experimental import pallas as pl
from jax.experimental.pallas import tpu as pltpu

D_MODEL = 2048
HEAD_DIM = 128
ATTN_HEADS = 8
ATTN_WIDTH = ATTN_HEADS * HEAD_DIM
GMLP_GROUPS = 8
GMLP_GROUP_DIM = 128
GMLP_WIDTH = GMLP_GROUPS * GMLP_GROUP_DIM
CHUNK = 128
ATTN_BLOCK = 128
DILATIONS = (1, 4, 16)
ROPE_THETA = 10000.0
D_FF = 4 * D_MODEL
NORM_EPS = 1e-6
LN_EPS = 1e-5

V7X_VMEM_LIMIT_BYTES = 56 * 1024 * 1024

PROJ_TM = 512
GMLP_TM = 1024
OUT_TM = 512
MLP_TM = 512
MLP_TF = 1024
MERGE_ROWS = 256
MAX_SINGLE_ACCESS_STRIDE = 4
STAGING_SLOT = len(DILATIONS)

F32 = jnp.float32
BF16 = jnp.bfloat16


def _rms_normalize(x, gain):
    return x * lax.rsqrt(jnp.mean(x * x, axis=-1, keepdims=True) + NORM_EPS) * gain


def _params(semantics):
    return pltpu.CompilerParams(dimension_semantics=semantics,
                                vmem_limit_bytes=V7X_VMEM_LIMIT_BYTES)


def _resident(shape, index_map):
    return pl.BlockSpec(shape, index_map, pipeline_mode=pl.Buffered(1))


def _qkv_proj_kernel(xn_ref, wq_ref, wk_ref, wv_ref, cos_ref, sin_ref,
                     nat_ref, d4_ref, d16_ref, *stage_refs):
    tm = xn_ref.shape[0]
    quarter = tm // 4
    xn = xn_ref[...]
    jobs = ((wq_ref, True), (wk_ref, True), (wv_ref, False))
    for t, (w_ref, rotate) in enumerate(jobs):
        head_ref, head4_ref = stage_refs[2 * t], stage_refs[2 * t + 1]
        z = jnp.dot(xn, w_ref[...], preferred_element_type=F32)
        for h in range(ATTN_HEADS):
            y = z[:, h * HEAD_DIM:(h + 1) * HEAD_DIM]
            if rotate:
                y = y * cos_ref[...] + pltpu.roll(y, HEAD_DIM // 2, 1) * sin_ref[...]
            slab = t * ATTN_HEADS + h
            nat_ref[slab] = y.astype(BF16)
            head_ref[h] = y
            for r4 in range(4):
                y4 = head_ref[h, pl.ds(r4, quarter, stride=4), :]
                d4_ref[slab, r4] = y4.astype(BF16)
                head4_ref[h, r4 * quarter:(r4 + 1) * quarter, :] = y4
            for r16 in range(16):
                c, r4 = divmod(r16, 4)
                y16 = head4_ref[h, pl.ds(r4 * quarter + c, tm // 16, stride=4), :]
                d16_ref[slab, r16] = y16.astype(BF16)


def _qkv_proj(xn, w_in_bf16, layer, cos_t, sin_t, batch, seq):
    m = xn.shape[0]
    tm = PROJ_TM
    nt = seq // tm
    slabs = 3 * ATTN_HEADS

    def dil_shape(d):
        return jax.ShapeDtypeStruct((batch, slabs, d, seq // d, HEAD_DIM), BF16)

    def dil_spec(d):
        return pl.BlockSpec((None, slabs, d, tm // d, HEAD_DIM),
                            lambda i: (i // nt, 0, 0, i % nt, 0))

    def w_spec(col_block):
        return _resident((None, D_MODEL, ATTN_WIDTH), lambda i: (layer, 0, col_block))

    table_spec = pl.BlockSpec((tm, HEAD_DIM), lambda i: (i % nt, 0))
    head_scratch = pltpu.VMEM((ATTN_HEADS, tm, HEAD_DIM), F32)
    return pl.pallas_call(
        _qkv_proj_kernel,
        grid=(m // tm,),
        in_specs=[pl.BlockSpec((tm, D_MODEL), lambda i: (i, 0)),
                  w_spec(0), w_spec(1), w_spec(2), table_spec, table_spec],
        out_specs=[pl.BlockSpec((None, slabs, tm, HEAD_DIM), lambda i: (i // nt, 0, i % nt, 0)),
                   dil_spec(4), dil_spec(16)],
        out_shape=[jax.ShapeDtypeStruct((batch, slabs, seq, HEAD_DIM), BF16),
                   dil_shape(4), dil_shape(16)],
        scratch_shapes=[head_scratch] * 6,
        compiler_params=_params(("arbitrary",)),
        name="qkv_proj",
    )(xn, w_in_bf16, w_in_bf16, w_in_bf16, cos_t, sin_t)


def _gmlp_body(xn, wu_ref, wv_ref, lng_ref, lnb_ref, ws_ref, bs_ref, out_ref):
    tm = xn.shape[0]
    a = jax.nn.gelu(jnp.dot(xn, wv_ref[...], preferred_element_type=F32))
    u = jax.nn.gelu(jnp.dot(xn, wu_ref[...], preferred_element_type=F32))
    row = lax.broadcasted_iota(jnp.int32, (CHUNK, CHUNK), 0)
    col = lax.broadcasted_iota(jnp.int32, (CHUNK, CHUNK), 1)
    causal = col <= row
    for g in range(GMLP_GROUPS):
        cols = slice(g * GMLP_GROUP_DIM, (g + 1) * GMLP_GROUP_DIM)
        ag = a[:, cols]
        mu = jnp.mean(ag, axis=-1, keepdims=True)
        xc = ag - mu
        var = jnp.mean(xc * xc, axis=-1, keepdims=True)
        vn = (xc * lax.rsqrt(var + LN_EPS) * lng_ref[:, cols] + lnb_ref[:, cols]).astype(BF16)
        w = jnp.where(causal, ws_ref[g], 0.0).astype(BF16)
        bias = bs_ref[:, g:g + 1]
        for c in range(tm // CHUNK):
            rows = slice(c * CHUNK, (c + 1) * CHUNK)
            sp = jnp.dot(w, vn[rows], preferred_element_type=F32) + bias
            out_ref[rows, cols] = (u[rows, cols] * sp).astype(out_ref.dtype)


def _gmlp_kernel(xn_ref, *refs):
    _gmlp_body(xn_ref[...], *refs)


def _gmlp_from_input_kernel(x_ref, g_ref, *refs):
    *body_refs, xn_ref = refs
    xn = _rms_normalize(x_ref[...], g_ref[...]).astype(BF16)
    xn_ref[...] = xn
    _gmlp_body(xn, *body_refs)


def _gmlp(x_or_xn, gain, w_in_bf16, layer, ln_g, ln_b, w_spatial, b_spatial_t):
    m = x_or_xn.shape[0]
    tm = GMLP_TM
    first_block = 3 * ATTN_WIDTH // GMLP_WIDTH
    row_spec = pl.BlockSpec((tm, D_MODEL), lambda i: (i, 0))
    vec_spec = pl.BlockSpec((1, GMLP_WIDTH), lambda i: (0, 0))
    in_specs = [_resident((None, D_MODEL, GMLP_WIDTH), lambda i: (layer, 0, first_block)),
                _resident((None, D_MODEL, GMLP_WIDTH), lambda i: (layer, 0, first_block + 1)),
                vec_spec, vec_spec,
                pl.BlockSpec((GMLP_GROUPS, CHUNK, CHUNK), lambda i: (0, 0, 0)),
                pl.BlockSpec((CHUNK, GMLP_GROUPS), lambda i: (0, 0))]
    operands = (w_in_bf16, w_in_bf16, ln_g, ln_b, w_spatial, b_spatial_t)
    out_specs = [pl.BlockSpec((tm, GMLP_WIDTH), lambda i: (i, 0))]
    out_shape = [jax.ShapeDtypeStruct((m, GMLP_WIDTH), BF16)]
    if gain is None:
        body, in_specs, operands = _gmlp_kernel, [row_spec] + in_specs, (x_or_xn,) + operands
    else:
        body = _gmlp_from_input_kernel
        in_specs = [row_spec, pl.BlockSpec((1, D_MODEL), lambda i: (0, 0))] + in_specs
        operands = (x_or_xn, gain) + operands
        out_specs.append(row_spec)
        out_shape.append(jax.ShapeDtypeStruct((m, D_MODEL), BF16))
    return pl.pallas_call(
        body,
        grid=(m // tm,),
        in_specs=in_specs,
        out_specs=out_specs,
        out_shape=out_shape,
        compiler_params=_params(("arbitrary",)),
        name="gmlp",
    )(*operands)


def _attention_kernel(q1_ref, k1_ref, v1_ref, q4_ref, k4_ref, v4_ref,
                      q16_ref, k16_ref, v16_ref, out_ref, acc_ref, max_ref, sum_ref):
    seq = out_ref.shape[0]
    blk = ATTN_BLOCK
    exp2_scale = HEAD_DIM ** -0.5 * math.log2(math.e)
    row = lax.broadcasted_iota(jnp.int32, (blk, blk), 0)
    col = lax.broadcasted_iota(jnp.int32, (blk, blk), 1)
    cur_mask = col <= row
    prev_mask = col >= row
    nt_dims = (((1,), (1,)), ((), ()))
    ones = jnp.ones((2 * blk, HEAD_DIM), BF16)

    def attend(br, d, q_ref, k_ref, v_ref, idx):
        nb = seq // (d * blk)
        n, r = idx % nb, idx // nb
        q = q_ref[pl.ds(idx * blk, blk), :]
        if n == 0:
            n_keys = blk
            keys = pl.ds(idx * blk, n_keys)
            s = lax.dot_general(q, k_ref[keys, :], nt_dims, preferred_element_type=F32)
            s = jnp.where(cur_mask, s, -jnp.inf)
            m = jnp.max(s, axis=-1, keepdims=True)
            p = jnp.exp2((s - m) * exp2_scale).astype(BF16)
        else:
            n_keys = 2 * blk
            keys = pl.ds((idx - 1) * blk, n_keys)
            s = lax.dot_general(q, k_ref[keys, :], nt_dims, preferred_element_type=F32)
            s_p = jnp.where(prev_mask, s[:, :blk], -jnp.inf)
            s_c = jnp.where(cur_mask, s[:, blk:], -jnp.inf)
            m = jnp.max(jnp.maximum(s_c, s_p), axis=-1, keepdims=True)
            p = jnp.concatenate([jnp.exp2((s_p - m) * exp2_scale).astype(BF16),
                                 jnp.exp2((s_c - m) * exp2_scale).astype(BF16)], axis=1)
        v_ext = jnp.concatenate([v_ref[keys, :], ones[:n_keys]], axis=1)
        acc = jnp.dot(p, v_ext, preferred_element_type=F32)
        if d <= MAX_SINGLE_ACCESS_STRIDE:
            dst = br
            rows = pl.ds(n * (blk * d) + r, blk, stride=d) if d > 1 else pl.ds(n * blk, blk)
        else:
            dst = STAGING_SLOT
            rows = pl.ds((r % 4) * (seq // 4) + (n * blk * d + r) // 4, blk, stride=d // 4)
        acc_ref[dst, rows, :] = acc[:, :HEAD_DIM]
        sum_ref[dst, rows, :] = acc[:, HEAD_DIM:]
        max_ref[dst, rows, :] = jnp.broadcast_to(m, (blk, HEAD_DIM))

    branches = ((1, q1_ref, k1_ref, v1_ref),
                (4, q4_ref, k4_ref, v4_ref),
                (16, q16_ref, k16_ref, v16_ref))
    for br, (d, q_ref, k_ref, v_ref) in enumerate(branches):
        for idx in range(seq // blk):
            attend(br, d, q_ref, k_ref, v_ref, idx)
        if d > MAX_SINGLE_ACCESS_STRIDE:
            for ref in (acc_ref, sum_ref, max_ref):
                for r4 in range(4):
                    for part in range(seq // (4 * MERGE_ROWS)):
                        u0 = part * MERGE_ROWS
                        src = pl.ds(r4 * (seq // 4) + u0, MERGE_ROWS)
                        dst_rows = pl.ds(4 * u0 + r4, MERGE_ROWS, stride=4)
                        ref[br, dst_rows, :] = ref[STAGING_SLOT, src, :]

    def merge(c, carry):
        rows = pl.ds(pl.multiple_of(c * MERGE_ROWS, MERGE_ROWS), MERGE_ROWS)
        m0, m1, m2 = max_ref[0, rows, :], max_ref[1, rows, :], max_ref[2, rows, :]
        top = jnp.maximum(jnp.maximum(m0, m1), m2)
        e0 = jnp.exp2((m0 - top) * exp2_scale)
        e1 = jnp.exp2((m1 - top) * exp2_scale)
        e2 = jnp.exp2((m2 - top) * exp2_scale)
        den = e0 * sum_ref[0, rows, :] + e1 * sum_ref[1, rows, :] + e2 * sum_ref[2, rows, :]
        num = e0 * acc_ref[0, rows, :] + e1 * acc_ref[1, rows, :] + e2 * acc_ref[2, rows, :]
        out_ref[rows, :] = (num / den).astype(out_ref.dtype)
        return carry

    lax.fori_loop(0, seq // MERGE_ROWS, merge, 0)


def _attention(qkv_by_order, batch, seq):
    def spec(t):
        return pl.BlockSpec((None, None, seq, HEAD_DIM),
                            lambda b, h: (b, t * ATTN_HEADS + h, 0, 0))

    out_spec = pl.BlockSpec((None, seq, HEAD_DIM), lambda b, h: (b, 0, h))
    stat = pltpu.VMEM((len(DILATIONS) + 1, seq, HEAD_DIM), F32)
    return pl.pallas_call(
        _attention_kernel,
        grid=(batch, ATTN_HEADS),
        in_specs=[spec(0), spec(1), spec(2)] * len(DILATIONS),
        out_specs=out_spec,
        out_shape=jax.ShapeDtypeStruct((batch, seq, ATTN_WIDTH), BF16),
        scratch_shapes=[stat, stat, stat],
        compiler_params=_params(("arbitrary", "arbitrary")),
        name="dilated_attention",
    )(*[a for a in qkv_by_order for _ in range(3)])


def _out_proj_kernel(attn_ref, gm_ref, wa_ref, wg_ref, x_ref, g_ref, x1_ref, xn_ref):
    x1 = (x_ref[...]
          + jnp.dot(attn_ref[...], wa_ref[...], preferred_element_type=F32)
          + jnp.dot(gm_ref[...], wg_ref[...], preferred_element_type=F32))
    x1_ref[...] = x1
    xn_ref[...] = _rms_normalize(x1, g_ref[...]).astype(BF16)


def _out_proj(attn, gm, w_out_bf16, layer, x2d, gain):
    m = x2d.shape[0]
    tm = OUT_TM
    gm_row_block = ATTN_WIDTH // GMLP_WIDTH
    row_spec = pl.BlockSpec((tm, D_MODEL), lambda i: (i, 0))
    return pl.pallas_call(
        _out_proj_kernel,
        grid=(m // tm,),
        in_specs=[
            pl.BlockSpec((tm, ATTN_WIDTH), lambda i: (i, 0)),
            pl.BlockSpec((tm, GMLP_WIDTH), lambda i: (i, 0)),
            _resident((None, ATTN_WIDTH, D_MODEL), lambda i: (layer, 0, 0)),
            _resident((None, GMLP_WIDTH, D_MODEL), lambda i: (layer, gm_row_block, 0)),
            row_spec,
            pl.BlockSpec((1, D_MODEL), lambda i: (0, 0)),
        ],
        out_specs=[row_spec, row_spec],
        out_shape=[jax.ShapeDtypeStruct((m, D_MODEL), F32),
                   jax.ShapeDtypeStruct((m, D_MODEL), BF16)],
        compiler_params=_params(("arbitrary",)),
        name="out_proj",
    )(attn, gm, w_out_bf16, w_out_bf16, x2d, gain)


def _mlp_kernel(x_ref, xn_ref, wu_ref, wd_ref, g_ref, *out_refs, last_layer):
    f = pl.program_id(1)
    acc_ref = out_refs[0]

    @pl.when(f == 0)
    def _():
        acc_ref[...] = x_ref[...]

    h = jnp.dot(xn_ref[...], wu_ref[...], preferred_element_type=F32)
    h = jnp.square(jnp.maximum(h, 0.0)).astype(BF16)
    acc_ref[...] += jnp.dot(h, wd_ref[...], preferred_element_type=F32)

    @pl.when(f == pl.num_programs(1) - 1)
    def _():
        normed = _rms_normalize(acc_ref[...], g_ref[...])
        if last_layer:
            acc_ref[...] = normed
        else:
            out_refs[1][...] = normed.astype(BF16)


def _mlp(x1, xn, w_up_bf16, w_down_bf16, layer, next_gain, last_layer):
    m = x1.shape[0]
    tm, tf = MLP_TM, MLP_TF
    row_spec = pl.BlockSpec((tm, D_MODEL), lambda i, f: (i, 0))
    out_specs = [row_spec]
    out_shape = [jax.ShapeDtypeStruct((m, D_MODEL), F32)]
    if not last_layer:
        out_specs.append(row_spec)
        out_shape.append(jax.ShapeDtypeStruct((m, D_MODEL), BF16))
    return pl.pallas_call(
        functools.partial(_mlp_kernel, last_layer=last_layer),
        grid=(m // tm, D_FF // tf),
        in_specs=[
            row_spec,
            row_spec,
            pl.BlockSpec((None, D_MODEL, tf), lambda i, f: (layer, 0, f)),
            pl.BlockSpec((None, tf, D_MODEL), lambda i, f: (layer, f, 0)),
            pl.BlockSpec((1, D_MODEL), lambda i, f: (0, 0)),
        ],
        out_specs=out_specs,
        out_shape=out_shape,
        compiler_params=_params(("arbitrary", "arbitrary")),
        name="mlp",
    )(x1, xn, w_up_bf16, w_down_bf16, next_gain)


def _rotary_tables(seq):
    half = HEAD_DIM // 2
    inv_freq = ROPE_THETA ** (-jnp.arange(half, dtype=F32) / half)
    ang = jnp.arange(seq, dtype=jnp.int32).astype(F32)[:, None] * inv_freq[None, :]
    cos, sin = jnp.cos(ang), jnp.sin(ang)
    return jnp.concatenate([cos, cos], axis=-1), jnp.concatenate([-sin, sin], axis=-1)


def kernel(x, norm1_g, w_in, gmlp_ln_g, gmlp_ln_b, w_spatial, b_spatial, w_out,
           norm2_g, w_up, w_down, final_g):
    batch, seq, _ = x.shape
    depth = w_in.shape[0]
    m = batch * seq
    cos_t, sin_t = _rotary_tables(seq)
    gains1 = norm1_g.reshape(depth, 1, D_MODEL)
    gains2 = norm2_g.reshape(depth, 1, D_MODEL)
    final_gain = final_g.reshape(1, D_MODEL)

    w_in_bf16, w_out_bf16 = w_in.astype(BF16), w_out.astype(BF16)
    w_up_bf16, w_down_bf16 = w_up.astype(BF16), w_down.astype(BF16)

    x2d = x.reshape(m, D_MODEL)
    xn = None
    for l in range(depth):
        last_layer = l == depth - 1
        gmlp_args = (w_in_bf16, l, gmlp_ln_g[l].reshape(1, GMLP_WIDTH),
                     gmlp_ln_b[l].reshape(1, GMLP_WIDTH), w_spatial[l], jnp.transpose(b_spatial[l]))
        if xn is None:
            gm, xn = _gmlp(x2d, gains1[l], *gmlp_args)
        else:
            gm, = _gmlp(xn, None, *gmlp_args)
        qkv = [a.reshape(batch, 3 * ATTN_HEADS, seq, HEAD_DIM)
               for a in _qkv_proj(xn, w_in_bf16, l, cos_t, sin_t, batch, seq)]
        attn = _attention(qkv, batch, seq).reshape(m, ATTN_WIDTH)
        x1, xn2 = _out_proj(attn, gm, w_out_bf16, l, x2d, gains2[l])
        next_gain = final_gain if last_layer else gains1[l + 1]
        res = _mlp(x1, xn2, w_up_bf16, w_down_bf16, l, next_gain, last_layer)
        if last_layer:
            x2d = res[0]
        else:
            x2d, xn = res
    return x2d.reshape(batch, seq, D_MODEL)
```

```python
import functools
import math

import jax
import jax.numpy as jnp
from jax import lax
from jax.experimental import pallas as pl
from jax.experimental.pallas import tpu as pltpu

D_MODEL = 2048
HEAD_DIM = 128
ATTN_HEADS = 8
ATTN_WIDTH = ATTN_HEADS * HEAD_DIM
GMLP_GROUPS = 8
GMLP_GROUP_DIM = 128
GMLP_WIDTH = GMLP_GROUPS * GMLP_GROUP_DIM
CHUNK = 128
ATTN_BLOCK = 128
DILATIONS = (1, 4, 16)
ROPE_THETA = 10000.0
D_FF = 4 * D_MODEL
NORM_EPS = 1e-6
LN_EPS = 1e-5

V7X_VMEM_LIMIT_BYTES = 56 * 1024 * 1024

PROJ_TM = 512
GMLP_TM = 1024
OUT_TM = 512
MLP_TM = 512
MLP_TF = 1024
MERGE_ROWS = 512
MAX_SINGLE_ACCESS_STRIDE = 4
STAGING_SLOT = len(DILATIONS)

F32 = jnp.float32
BF16 = jnp.bfloat16


def _rms_normalize(x, gain):
    return x * lax.rsqrt(jnp.mean(x * x, axis=-1, keepdims=True) + NORM_EPS) * gain


def _params(semantics):
    return pltpu.CompilerParams(dimension_semantics=semantics,
                                vmem_limit_bytes=V7X_VMEM_LIMIT_BYTES)


def _resident(shape, index_map):
    return pl.BlockSpec(shape, index_map, pipeline_mode=pl.Buffered(1))


def _qkv_proj_kernel(xn_ref, wq_ref, wk_ref, wv_ref, cos_ref, sin_ref,
                     nat_ref, d4_ref, d16_ref, *stage_refs):
    tm = xn_ref.shape[0]
    quarter = tm // 4
    xn = xn_ref[...]
    jobs = ((wq_ref, True), (wk_ref, True), (wv_ref, False))
    for t, (w_ref, rotate) in enumerate(jobs):
        head_ref, head4_ref = stage_refs[2 * t], stage_refs[2 * t + 1]
        z = jnp.dot(xn, w_ref[...], preferred_element_type=F32)
        for h in range(ATTN_HEADS):
            y = z[:, h * HEAD_DIM:(h + 1) * HEAD_DIM]
            if rotate:
                y = y * cos_ref[...] + pltpu.roll(y, HEAD_DIM // 2, 1) * sin_ref[...]
            slab = t * ATTN_HEADS + h
            nat_ref[slab] = y.astype(BF16)
            head_ref[h] = y
            for r4 in range(4):
                y4 = head_ref[h, pl.ds(r4, quarter, stride=4), :]
                d4_ref[slab, r4] = y4.astype(BF16)
                head4_ref[h, r4 * quarter:(r4 + 1) * quarter, :] = y4
            for r16 in range(16):
                c, r4 = divmod(r16, 4)
                y16 = head4_ref[h, pl.ds(r4 * quarter + c, tm // 16, stride=4), :]
                d16_ref[slab, r16] = y16.astype(BF16)


def _qkv_proj(xn, w_in_bf16, layer, cos_t, sin_t, batch, seq):
    m = xn.shape[0]
    tm = PROJ_TM
    nt = seq // tm
    slabs = 3 * ATTN_HEADS

    def dil_shape(d):
        return jax.ShapeDtypeStruct((batch, slabs, d, seq // d, HEAD_DIM), BF16)

    def dil_spec(d):
        return pl.BlockSpec((None, slabs, d, tm // d, HEAD_DIM),
                            lambda i: (i // nt, 0, 0, i % nt, 0))

    def w_spec(col_block):
        return _resident((None, D_MODEL, ATTN_WIDTH), lambda i: (layer, 0, col_block))

    table_spec = pl.BlockSpec((tm, HEAD_DIM), lambda i: (i % nt, 0))
    head_scratch = pltpu.VMEM((ATTN_HEADS, tm, HEAD_DIM), F32)
    return pl.pallas_call(
        _qkv_proj_kernel,
        grid=(m // tm,),
        in_specs=[pl.BlockSpec((tm, D_MODEL), lambda i: (i, 0)),
                  w_spec(0), w_spec(1), w_spec(2), table_spec, table_spec],
        out_specs=[pl.BlockSpec((None, slabs, tm, HEAD_DIM), lambda i: (i // nt, 0, i % nt, 0)),
                   dil_spec(4), dil_spec(16)],
        out_shape=[jax.ShapeDtypeStruct((batch, slabs, seq, HEAD_DIM), BF16),
                   dil_shape(4), dil_shape(16)],
        scratch_shapes=[head_scratch] * 6,
        compiler_params=_params(("arbitrary",)),
        name="qkv_proj",
    )(xn, w_in_bf16, w_in_bf16, w_in_bf16, cos_t, sin_t)


def _gmlp_body(xn, wu_ref, wv_ref, lng_ref, lnb_ref, ws_ref, bs_ref, out_ref):
    tm = xn.shape[0]
    a = jax.nn.gelu(jnp.dot(xn, wv_ref[...], preferred_element_type=F32))
    u = jax.nn.gelu(jnp.dot(xn, wu_ref[...], preferred_element_type=F32))
    row = lax.broadcasted_iota(jnp.int32, (CHUNK, CHUNK), 0)
    col = lax.broadcasted_iota(jnp.int32, (CHUNK, CHUNK), 1)
    causal = col <= row
    for g in range(GMLP_GROUPS):
        cols = slice(g * GMLP_GROUP_DIM, (g + 1) * GMLP_GROUP_DIM)
        ag = a[:, cols]
        mu = jnp.mean(ag, axis=-1, keepdims=True)
        xc = ag - mu
        var = jnp.mean(xc * xc, axis=-1, keepdims=True)
        vn = (xc * lax.rsqrt(var + LN_EPS) * lng_ref[:, cols] + lnb_ref[:, cols]).astype(BF16)
        w = jnp.where(causal, ws_ref[g], 0.0).astype(BF16)
        bias = bs_ref[:, g:g + 1]
        for c in range(tm // CHUNK):
            rows = slice(c * CHUNK, (c + 1) * CHUNK)
            sp = jnp.dot(w, vn[rows], preferred_element_type=F32) + bias
            out_ref[rows, cols] = (u[rows, cols] * sp).astype(out_ref.dtype)


def _gmlp_kernel(xn_ref, *refs):
    _gmlp_body(xn_ref[...], *refs)


def _gmlp_from_input_kernel(x_ref, g_ref, *refs):
    *body_refs, xn_ref = refs
    xn = _rms_normalize(x_ref[...], g_ref[...]).astype(BF16)
    xn_ref[...] = xn
    _gmlp_body(xn, *body_refs)


def _gmlp(x_or_xn, gain, w_in_bf16, layer, ln_g, ln_b, w_spatial, b_spatial_t):
    m = x_or_xn.shape[0]
    tm = GMLP_TM
    first_block = 3 * ATTN_WIDTH // GMLP_WIDTH
    row_spec = pl.BlockSpec((tm, D_MODEL), lambda i: (i, 0))
    vec_spec = pl.BlockSpec((1, GMLP_WIDTH), lambda i: (0, 0))
    in_specs = [_resident((None, D_MODEL, GMLP_WIDTH), lambda i: (layer, 0, first_block)),
                _resident((None, D_MODEL, GMLP_WIDTH), lambda i: (layer, 0, first_block + 1)),
                vec_spec, vec_spec,
                pl.BlockSpec((GMLP_GROUPS, CHUNK, CHUNK), lambda i: (0, 0, 0)),
                pl.BlockSpec((CHUNK, GMLP_GROUPS), lambda i: (0, 0))]
    operands = (w_in_bf16, w_in_bf16, ln_g, ln_b, w_spatial, b_spatial_t)
    out_specs = [pl.BlockSpec((tm, GMLP_WIDTH), lambda i: (i, 0))]
    out_shape = [jax.ShapeDtypeStruct((m, GMLP_WIDTH), BF16)]
    if gain is None:
        body, in_specs, operands = _gmlp_kernel, [row_spec] + in_specs, (x_or_xn,) + operands
    else:
        body = _gmlp_from_input_kernel
        in_specs = [row_spec, pl.BlockSpec((1, D_MODEL), lambda i: (0, 0))] + in_specs
        operands = (x_or_xn, gain) + operands
        out_specs.append(row_spec)
        out_shape.append(jax.ShapeDtypeStruct((m, D_MODEL), BF16))
    return pl.pallas_call(
        body,
        grid=(m // tm,),
        in_specs=in_specs,
        out_specs=out_specs,
        out_shape=out_shape,
        compiler_params=_params(("arbitrary",)),
        name="gmlp",
    )(*operands)


def _attention_kernel(q1_ref, k1_ref, v1_ref, q4_ref, k4_ref, v4_ref,
                      q16_ref, k16_ref, v16_ref, out_ref, acc_ref, max_ref, sum_ref):
    seq = out_ref.shape[0]
    blk = ATTN_BLOCK
    exp2_scale = HEAD_DIM ** -0.5 * math.log2(math.e)
    row = lax.broadcasted_iota(jnp.int32, (blk, blk), 0)
    col = lax.broadcasted_iota(jnp.int32, (blk, blk), 1)
    cur_mask = col <= row
    prev_mask = col >= row
    nt_dims = (((1,), (1,)), ((), ()))
    ones = jnp.ones((2 * blk, HEAD_DIM), BF16)

    def attend(br, d, q_ref, k_ref, v_ref, idx):
        nb = seq // (d * blk)
        n, r = idx % nb, idx // nb
        q = q_ref[pl.ds(idx * blk, blk), :]
        if n == 0:
            n_keys = blk
            keys = pl.ds(idx * blk, n_keys)
            s = lax.dot_general(q, k_ref[keys, :], nt_dims, preferred_element_type=F32)
            s = jnp.where(cur_mask, s, -jnp.inf)
            m = jnp.max(s, axis=-1, keepdims=True)
            p = jnp.exp2((s - m) * exp2_scale).astype(BF16)
        else:
            n_keys = 2 * blk
            keys = pl.ds((idx - 1) * blk, n_keys)
            s = lax.dot_general(q, k_ref[keys, :], nt_dims, preferred_element_type=F32)
            s_p = jnp.where(prev_mask, s[:, :blk], -jnp.inf)
            s_c = jnp.where(cur_mask, s[:, blk:], -jnp.inf)
            m = jnp.max(jnp.maximum(s_c, s_p), axis=-1, keepdims=True)
            p = jnp.concatenate([jnp.exp2((s_p - m) * exp2_scale).astype(BF16),
                                 jnp.exp2((s_c - m) * exp2_scale).astype(BF16)], axis=1)
        v_ext = jnp.concatenate([v_ref[keys, :], ones[:n_keys]], axis=1)
        acc = jnp.dot(p, v_ext, preferred_element_type=F32)
        if d <= MAX_SINGLE_ACCESS_STRIDE:
            dst = br
            rows = pl.ds(n * (blk * d) + r, blk, stride=d) if d > 1 else pl.ds(n * blk, blk)
        else:
            dst = STAGING_SLOT
            rows = pl.ds((r % 4) * (seq // 4) + (n * blk * d + r) // 4, blk, stride=d // 4)
        acc_ref[dst, rows, :] = acc[:, :HEAD_DIM]
        sum_ref[dst, rows, :] = acc[:, HEAD_DIM:]
        max_ref[dst, rows, :] = jnp.broadcast_to(m, (blk, HEAD_DIM))

    branches = ((1, q1_ref, k1_ref, v1_ref),
                (4, q4_ref, k4_ref, v4_ref),
                (16, q16_ref, k16_ref, v16_ref))
    for br, (d, q_ref, k_ref, v_ref) in enumerate(branches):
        for idx in range(seq // blk):
            attend(br, d, q_ref, k_ref, v_ref, idx)
        if d > MAX_SINGLE_ACCESS_STRIDE:
            for ref in (acc_ref, sum_ref, max_ref):
                for r4 in range(4):
                    for part in range(seq // (4 * MERGE_ROWS)):
                        u0 = part * MERGE_ROWS
                        src = pl.ds(r4 * (seq // 4) + u0, MERGE_ROWS)
                        dst_rows = pl.ds(4 * u0 + r4, MERGE_ROWS, stride=4)
                        ref[br, dst_rows, :] = ref[STAGING_SLOT, src, :]

    def merge(c, carry):
        rows = pl.ds(pl.multiple_of(c * MERGE_ROWS, MERGE_ROWS), MERGE_ROWS)
        m0, m1, m2 = max_ref[0, rows, :], max_ref[1, rows, :], max_ref[2, rows, :]
        top = jnp.maximum(jnp.maximum(m0, m1), m2)
        e0 = jnp.exp2((m0 - top) * exp2_scale)
        e1 = jnp.exp2((m1 - top) * exp2_scale)
        e2 = jnp.exp2((m2 - top) * exp2_scale)
        den = e0 * sum_ref[0, rows, :] + e1 * sum_ref[1, rows, :] + e2 * sum_ref[2, rows, :]
        num = e0 * acc_ref[0, rows, :] + e1 * acc_ref[1, rows, :] + e2 * acc_ref[2, rows, :]
        out_ref[rows, :] = (num / den).astype(out_ref.dtype)
        return carry

    lax.fori_loop(0, seq // MERGE_ROWS, merge, 0)


def _attention(qkv_by_order, batch, seq):
    def spec(t):
        return pl.BlockSpec((None, None, seq, HEAD_DIM),
                            lambda b, h: (b, t * ATTN_HEADS + h, 0, 0))

    out_spec = pl.BlockSpec((None, seq, HEAD_DIM), lambda b, h: (b, 0, h))
    stat = pltpu.VMEM((len(DILATIONS) + 1, seq, HEAD_DIM), F32)
    return pl.pallas_call(
        _attention_kernel,
        grid=(batch, ATTN_HEADS),
        in_specs=[spec(0), spec(1), spec(2)] * len(DILATIONS),
        out_specs=out_spec,
        out_shape=jax.ShapeDtypeStruct((batch, seq, ATTN_WIDTH), BF16),
        scratch_shapes=[stat, stat, stat],
        compiler_params=_params(("arbitrary", "arbitrary")),
        name="dilated_attention",
    )(*[a for a in qkv_by_order for _ in range(3)])


def _out_proj_kernel(attn_ref, gm_ref, w_ref, x_ref, g_ref, x1_ref, xn_ref):
    mix = jnp.concatenate([attn_ref[...], gm_ref[...]], axis=1)
    x1 = x_ref[...] + jnp.dot(mix, w_ref[...], preferred_element_type=F32)
    x1_ref[...] = x1
    xn_ref[...] = _rms_normalize(x1, g_ref[...]).astype(BF16)


def _out_proj(attn, gm, w_out_bf16, layer, x2d, gain):
    m = x2d.shape[0]
    tm = OUT_TM
    row_spec = pl.BlockSpec((tm, D_MODEL), lambda i: (i, 0))
    return pl.pallas_call(
        _out_proj_kernel,
        grid=(m // tm,),
        in_specs=[
            pl.BlockSpec((tm, ATTN_WIDTH), lambda i: (i, 0)),
            pl.BlockSpec((tm, GMLP_WIDTH), lambda i: (i, 0)),
            _resident((None, ATTN_WIDTH + GMLP_WIDTH, D_MODEL), lambda i: (layer, 0, 0)),
            row_spec,
            pl.BlockSpec((1, D_MODEL), lambda i: (0, 0)),
        ],
        out_specs=[row_spec, row_spec],
        out_shape=[jax.ShapeDtypeStruct((m, D_MODEL), F32),
                   jax.ShapeDtypeStruct((m, D_MODEL), BF16)],
        compiler_params=_params(("arbitrary",)),
        name="out_proj",
    )(attn, gm, w_out_bf16, x2d, gain)


def _mlp_kernel(x_ref, xn_ref, wu_ref, wd_ref, g_ref, *out_refs, last_layer):
    f = pl.program_id(1)
    acc_ref = out_refs[0]

    @pl.when(f == 0)
    def _():
        acc_ref[...] = x_ref[...]

    h = jnp.dot(xn_ref[...], wu_ref[...], preferred_element_type=F32)
    h = jnp.square(jnp.maximum(h, 0.0)).astype(BF16)
    acc_ref[...] += jnp.dot(h, wd_ref[...], preferred_element_type=F32)

    @pl.when(f == pl.num_programs(1) - 1)
    def _():
        normed = _rms_normalize(acc_ref[...], g_ref[...])
        if last_layer:
            acc_ref[...] = normed
        else:
            out_refs[1][...] = normed.astype(BF16)


def _mlp(x1, xn, w_up_bf16, w_down_bf16, layer, next_gain, last_layer):
    m = x1.shape[0]
    tm, tf = MLP_TM, MLP_TF
    row_spec = pl.BlockSpec((tm, D_MODEL), lambda i, f: (i, 0))
    out_specs = [row_spec]
    out_shape = [jax.ShapeDtypeStruct((m, D_MODEL), F32)]
    if not last_layer:
        out_specs.append(row_spec)
        out_shape.append(jax.ShapeDtypeStruct((m, D_MODEL), BF16))
    return pl.pallas_call(
        functools.partial(_mlp_kernel, last_layer=last_layer),
        grid=(m // tm, D_FF // tf),
        in_specs=[
            row_spec,
            row_spec,
            pl.BlockSpec((None, D_MODEL, tf), lambda i, f: (layer, 0, f)),
            pl.BlockSpec((None, tf, D_MODEL), lambda i, f: (layer, f, 0)),
            pl.BlockSpec((1, D_MODEL), lambda i, f: (0, 0)),
        ],
        out_specs=out_specs,
        out_shape=out_shape,
        compiler_params=_params(("arbitrary", "arbitrary")),
        name="mlp",
    )(x1, xn, w_up_bf16, w_down_bf16, next_gain)


def _rotary_tables(seq):
    half = HEAD_DIM // 2
    inv_freq = ROPE_THETA ** (-jnp.arange(half, dtype=F32) / half)
    ang = jnp.arange(seq, dtype=jnp.int32).astype(F32)[:, None] * inv_freq[None, :]
    cos, sin = jnp.cos(ang), jnp.sin(ang)
    return jnp.concatenate([cos, cos], axis=-1), jnp.concatenate([-sin, sin], axis=-1)


def kernel(x, norm1_g, w_in, gmlp_ln_g, gmlp_ln_b, w_spatial, b_spatial, w_out,
           norm2_g, w_up, w_down, final_g):
    batch, seq, _ = x.shape
    depth = w_in.shape[0]
    m = batch * seq
    cos_t, sin_t = _rotary_tables(seq)
    gains1 = norm1_g.reshape(depth, 1, D_MODEL)
    gains2 = norm2_g.reshape(depth, 1, D_MODEL)
    final_gain = final_g.reshape(1, D_MODEL)

    w_in_bf16, w_out_bf16 = w_in.astype(BF16), w_out.astype(BF16)
    w_up_bf16, w_down_bf16 = w_up.astype(BF16), w_down.astype(BF16)

    x2d = x.reshape(m, D_MODEL)
    xn = None
    for l in range(depth):
        last_layer = l == depth - 1
        gmlp_args = (w_in_bf16, l, gmlp_ln_g[l].reshape(1, GMLP_WIDTH),
                     gmlp_ln_b[l].reshape(1, GMLP_WIDTH), w_spatial[l], jnp.transpose(b_spatial[l]))
        if xn is None:
            gm, xn = _gmlp(x2d, gains1[l], *gmlp_args)
        else:
            gm, = _gmlp(xn, None, *gmlp_args)
        qkv = [a.reshape(batch, 3 * ATTN_HEADS, seq, HEAD_DIM)
               for a in _qkv_proj(xn, w_in_bf16, l, cos_t, sin_t, batch, seq)]
        attn = _attention(qkv, batch, seq).reshape(m, ATTN_WIDTH)
        x1, xn2 = _out_proj(attn, gm, w_out_bf16, l, x2d, gains2[l])
        next_gain = final_gain if last_layer else gains1[l + 1]
        res = _mlp(x1, xn2, w_up_bf16, w_down_bf16, l, next_gain, last_layer)
        if last_layer:
            x2d = res[0]
        else:
            x2d, xn = res
    return x2d.reshape(batch, seq, D_MODEL)
```

```python
import functools
import math

import jax
import jax.numpy as jnp
from jax import lax
from jax.experimental import pallas as pl
from jax.experimental.pallas import tpu as pltpu

D_MODEL = 2048
HEAD_DIM = 128
ATTN_HEADS = 8
ATTN_WIDTH = ATTN_HEADS * HEAD_DIM
GMLP_GROUPS = 8
GMLP_GROUP_DIM = 128
GMLP_WIDTH = GMLP_GROUPS * GMLP_GROUP_DIM
CHUNK = 128
ATTN_BLOCK = 128
DILATIONS = (1, 4, 16)
ROPE_THETA = 10000.0
D_FF = 4 * D_MODEL
NORM_EPS = 1e-6
LN_EPS = 1e-5

V7X_VMEM_LIMIT_BYTES = 56 * 1024 * 1024

PROJ_TM = 512
GMLP_TM = 1024
OUT_TM = 512
MLP_TM = 512
MLP_TF = 1024
HOP_ROWS = 512
MAX_SINGLE_ACCESS_STRIDE = 4
STAGING_SLOT = len(DILATIONS) - 1

F32 = jnp.float32
BF16 = jnp.bfloat16


def _rms_normalize(x, gain):
    return x * lax.rsqrt(jnp.mean(x * x, axis=-1, keepdims=True) + NORM_EPS) * gain


def _params(semantics):
    return pltpu.CompilerParams(dimension_semantics=semantics,
                                vmem_limit_bytes=V7X_VMEM_LIMIT_BYTES)


def _resident(shape, index_map):
    return pl.BlockSpec(shape, index_map, pipeline_mode=pl.Buffered(1))


def _qkv_proj_kernel(xn_ref, wq_ref, wk_ref, wv_ref, cos_ref, sin_ref,
                     nat_ref, d4_ref, d16_ref, *stage_refs):
    tm = xn_ref.shape[0]
    quarter = tm // 4
    xn = xn_ref[...]
    jobs = ((wq_ref, True), (wk_ref, True), (wv_ref, False))
    for t, (w_ref, rotate) in enumerate(jobs):
        head_ref, head4_ref = stage_refs[2 * t], stage_refs[2 * t + 1]
        z = jnp.dot(xn, w_ref[...], preferred_element_type=F32)
        for h in range(ATTN_HEADS):
            y = z[:, h * HEAD_DIM:(h + 1) * HEAD_DIM]
            if rotate:
                y = y * cos_ref[...] + pltpu.roll(y, HEAD_DIM // 2, 1) * sin_ref[...]
            slab = t * ATTN_HEADS + h
            nat_ref[slab] = y.astype(BF16)
            head_ref[h] = y
            for r4 in range(4):
                y4 = head_ref[h, pl.ds(r4, quarter, stride=4), :]
                d4_ref[slab, r4] = y4.astype(BF16)
                head4_ref[h, r4 * quarter:(r4 + 1) * quarter, :] = y4
            for r16 in range(16):
                c, r4 = divmod(r16, 4)
                y16 = head4_ref[h, pl.ds(r4 * quarter + c, tm // 16, stride=4), :]
                d16_ref[slab, r16] = y16.astype(BF16)


def _qkv_proj(xn, w_in_bf16, layer, cos_t, sin_t, batch, seq):
    m = xn.shape[0]
    tm = PROJ_TM
    nt = seq // tm
    slabs = 3 * ATTN_HEADS

    def dil_shape(d):
        return jax.ShapeDtypeStruct((batch, slabs, d, seq // d, HEAD_DIM), BF16)

    def dil_spec(d):
        return pl.BlockSpec((None, slabs, d, tm // d, HEAD_DIM),
                            lambda i: (i // nt, 0, 0, i % nt, 0))

    def w_spec(col_block):
        return _resident((None, D_MODEL, ATTN_WIDTH), lambda i: (layer, 0, col_block))

    table_spec = pl.BlockSpec((tm, HEAD_DIM), lambda i: (i % nt, 0))
    head_scratch = pltpu.VMEM((ATTN_HEADS, tm, HEAD_DIM), F32)
    return pl.pallas_call(
        _qkv_proj_kernel,
        grid=(m // tm,),
        in_specs=[pl.BlockSpec((tm, D_MODEL), lambda i: (i, 0)),
                  w_spec(0), w_spec(1), w_spec(2), table_spec, table_spec],
        out_specs=[pl.BlockSpec((None, slabs, tm, HEAD_DIM), lambda i: (i // nt, 0, i % nt, 0)),
                   dil_spec(4), dil_spec(16)],
        out_shape=[jax.ShapeDtypeStruct((batch, slabs, seq, HEAD_DIM), BF16),
                   dil_shape(4), dil_shape(16)],
        scratch_shapes=[head_scratch] * 6,
        compiler_params=_params(("arbitrary",)),
        name="qkv_proj",
    )(xn, w_in_bf16, w_in_bf16, w_in_bf16, cos_t, sin_t)


def _gmlp_body(xn, wu_ref, wv_ref, lng_ref, lnb_ref, ws_ref, bs_ref, out_ref):
    tm = xn.shape[0]
    a = jax.nn.gelu(jnp.dot(xn, wv_ref[...], preferred_element_type=F32))
    u = jax.nn.gelu(jnp.dot(xn, wu_ref[...], preferred_element_type=F32))
    row = lax.broadcasted_iota(jnp.int32, (CHUNK, CHUNK), 0)
    col = lax.broadcasted_iota(jnp.int32, (CHUNK, CHUNK), 1)
    causal = col <= row
    for g in range(GMLP_GROUPS):
        cols = slice(g * GMLP_GROUP_DIM, (g + 1) * GMLP_GROUP_DIM)
        ag = a[:, cols]
        mu = jnp.mean(ag, axis=-1, keepdims=True)
        xc = ag - mu
        var = jnp.mean(xc * xc, axis=-1, keepdims=True)
        vn = (xc * lax.rsqrt(var + LN_EPS) * lng_ref[:, cols] + lnb_ref[:, cols]).astype(BF16)
        w = jnp.where(causal, ws_ref[g], 0.0).astype(BF16)
        bias = bs_ref[:, g:g + 1]
        for c in range(tm // CHUNK):
            rows = slice(c * CHUNK, (c + 1) * CHUNK)
            sp = jnp.dot(w, vn[rows], preferred_element_type=F32) + bias
            out_ref[rows, cols] = (u[rows, cols] * sp).astype(out_ref.dtype)


def _gmlp_kernel(xn_ref, *refs):
    _gmlp_body(xn_ref[...], *refs)


def _gmlp_from_input_kernel(x_ref, g_ref, *refs):
    *body_refs, xn_ref = refs
    xn = _rms_normalize(x_ref[...], g_ref[...]).astype(BF16)
    xn_ref[...] = xn
    _gmlp_body(xn, *body_refs)


def _gmlp(x_or_xn, gain, w_in_bf16, layer, ln_g, ln_b, w_spatial, b_spatial_t):
    m = x_or_xn.shape[0]
    tm = GMLP_TM
    first_block = 3 * ATTN_WIDTH // GMLP_WIDTH
    row_spec = pl.BlockSpec((tm, D_MODEL), lambda i: (i, 0))
    vec_spec = pl.BlockSpec((1, GMLP_WIDTH), lambda i: (0, 0))
    in_specs = [_resident((None, D_MODEL, GMLP_WIDTH), lambda i: (layer, 0, first_block)),
                _resident((None, D_MODEL, GMLP_WIDTH), lambda i: (layer, 0, first_block + 1)),
                vec_spec, vec_spec,
                pl.BlockSpec((GMLP_GROUPS, CHUNK, CHUNK), lambda i: (0, 0, 0)),
                pl.BlockSpec((CHUNK, GMLP_GROUPS), lambda i: (0, 0))]
    operands = (w_in_bf16, w_in_bf16, ln_g, ln_b, w_spatial, b_spatial_t)
    out_specs = [pl.BlockSpec((tm, GMLP_WIDTH), lambda i: (i, 0))]
    out_shape = [jax.ShapeDtypeStruct((m, GMLP_WIDTH), BF16)]
    if gain is None:
        body, in_specs, operands = _gmlp_kernel, [row_spec] + in_specs, (x_or_xn,) + operands
    else:
        body = _gmlp_from_input_kernel
        in_specs = [row_spec, pl.BlockSpec((1, D_MODEL), lambda i: (0, 0))] + in_specs
        operands = (x_or_xn, gain) + operands
        out_specs.append(row_spec)
        out_shape.append(jax.ShapeDtypeStruct((m, D_MODEL), BF16))
    return pl.pallas_call(
        body,
        grid=(m // tm,),
        in_specs=in_specs,
        out_specs=out_specs,
        out_shape=out_shape,
        compiler_params=_params(("arbitrary",)),
        name="gmlp",
    )(*operands)


def _attention_kernel(q1_ref, k1_ref, v1_ref, q4_ref, k4_ref, v4_ref,
                      q16_ref, k16_ref, v16_ref, out_ref, acc_ref, max_ref, sum_ref):
    seq = out_ref.shape[0]
    blk = ATTN_BLOCK
    exp2_scale = HEAD_DIM ** -0.5 * math.log2(math.e)
    row = lax.broadcasted_iota(jnp.int32, (blk, blk), 0)
    col = lax.broadcasted_iota(jnp.int32, (blk, blk), 1)
    cur_mask = col <= row
    prev_mask = col >= row
    nt_dims = (((1,), (1,)), ((), ()))
    ones = jnp.ones((2 * blk, HEAD_DIM), BF16)

    def attend(slot, d, q_ref, k_ref, v_ref, idx):
        nb = seq // (d * blk)
        n, r = idx % nb, idx // nb
        q = q_ref[pl.ds(idx * blk, blk), :]
        if n == 0:
            n_keys = blk
            keys = pl.ds(idx * blk, n_keys)
            s = lax.dot_general(q, k_ref[keys, :], nt_dims, preferred_element_type=F32)
            s = jnp.where(cur_mask, s, -jnp.inf)
            m = jnp.max(s, axis=-1, keepdims=True)
            p = jnp.exp2((s - m) * exp2_scale).astype(BF16)
        else:
            n_keys = 2 * blk
            keys = pl.ds((idx - 1) * blk, n_keys)
            s = lax.dot_general(q, k_ref[keys, :], nt_dims, preferred_element_type=F32)
            s_p = jnp.where(prev_mask, s[:, :blk], -jnp.inf)
            s_c = jnp.where(cur_mask, s[:, blk:], -jnp.inf)
            m = jnp.max(jnp.maximum(s_c, s_p), axis=-1, keepdims=True)
            p = jnp.concatenate([jnp.exp2((s_p - m) * exp2_scale).astype(BF16),
                                 jnp.exp2((s_c - m) * exp2_scale).astype(BF16)], axis=1)
        v_ext = jnp.concatenate([v_ref[keys, :], ones[:n_keys]], axis=1)
        acc = jnp.dot(p, v_ext, preferred_element_type=F32)
        a0, l0, m0 = acc[:, :HEAD_DIM], acc[:, HEAD_DIM:], jnp.broadcast_to(m, (blk, HEAD_DIM))
        if d == 1:
            rows = pl.ds(n * blk, blk)
            m1, m2 = max_ref[0, rows, :], max_ref[1, rows, :]
            top = jnp.maximum(jnp.maximum(m0, m1), m2)
            e0 = jnp.exp2((m0 - top) * exp2_scale)
            e1 = jnp.exp2((m1 - top) * exp2_scale)
            e2 = jnp.exp2((m2 - top) * exp2_scale)
            den = e0 * l0 + e1 * sum_ref[0, rows, :] + e2 * sum_ref[1, rows, :]
            num = e0 * a0 + e1 * acc_ref[0, rows, :] + e2 * acc_ref[1, rows, :]
            out_ref[rows, :] = (num / den).astype(out_ref.dtype)
            return
        if d <= MAX_SINGLE_ACCESS_STRIDE:
            rows = pl.ds(n * (blk * d) + r, blk, stride=d)
        else:
            slot = STAGING_SLOT
            rows = pl.ds((r % 4) * (seq // 4) + (n * blk * d + r) // 4, blk, stride=d // 4)
        acc_ref[slot, rows, :] = a0
        sum_ref[slot, rows, :] = l0
        max_ref[slot, rows, :] = m0

    dilated = ((0, 4, q4_ref, k4_ref, v4_ref), (1, 16, q16_ref, k16_ref, v16_ref))
    for slot, d, q_ref, k_ref, v_ref in dilated:
        for idx in range(seq // blk):
            attend(slot, d, q_ref, k_ref, v_ref, idx)
        if d > MAX_SINGLE_ACCESS_STRIDE:
            for ref in (acc_ref, sum_ref, max_ref):
                for r4 in range(4):
                    for part in range(seq // (4 * HOP_ROWS)):
                        u0 = part * HOP_ROWS
                        src = pl.ds(r4 * (seq // 4) + u0, HOP_ROWS)
                        ref[slot, pl.ds(4 * u0 + r4, HOP_ROWS, stride=4), :] = ref[STAGING_SLOT, src, :]
    for idx in range(seq // blk):
        attend(None, 1, q1_ref, k1_ref, v1_ref, idx)


def _attention(qkv_by_order, batch, seq):
    def spec(t):
        return pl.BlockSpec((None, None, seq, HEAD_DIM),
                            lambda b, h: (b, t * ATTN_HEADS + h, 0, 0))

    out_spec = pl.BlockSpec((None, seq, HEAD_DIM), lambda b, h: (b, 0, h))
    stat = pltpu.VMEM((STAGING_SLOT + 1, seq, HEAD_DIM), F32)
    return pl.pallas_call(
        _attention_kernel,
        grid=(batch, ATTN_HEADS),
        in_specs=[spec(0), spec(1), spec(2)] * len(DILATIONS),
        out_specs=out_spec,
        out_shape=jax.ShapeDtypeStruct((batch, seq, ATTN_WIDTH), BF16),
        scratch_shapes=[stat, stat, stat],
        compiler_params=_params(("arbitrary", "arbitrary")),
        name="dilated_attention",
    )(*[a for a in qkv_by_order for _ in range(3)])


def _out_proj_kernel(attn_ref, gm_ref, w_ref, x_ref, g_ref, x1_ref, xn_ref):
    mix = jnp.concatenate([attn_ref[...], gm_ref[...]], axis=1)
    x1 = x_ref[...] + jnp.dot(mix, w_ref[...], preferred_element_type=F32)
    x1_ref[...] = x1
    xn_ref[...] = _rms_normalize(x1, g_ref[...]).astype(BF16)


def _out_proj(attn, gm, w_out_bf16, layer, x2d, gain):
    m = x2d.shape[0]
    tm = OUT_TM
    row_spec = pl.BlockSpec((tm, D_MODEL), lambda i: (i, 0))
    return pl.pallas_call(
        _out_proj_kernel,
        grid=(m // tm,),
        in_specs=[
            pl.BlockSpec((tm, ATTN_WIDTH), lambda i: (i, 0)),
            pl.BlockSpec((tm, GMLP_WIDTH), lambda i: (i, 0)),
            _resident((None, ATTN_WIDTH + GMLP_WIDTH, D_MODEL), lambda i: (layer, 0, 0)),
            row_spec,
            pl.BlockSpec((1, D_MODEL), lambda i: (0, 0)),
        ],
        out_specs=[row_spec, row_spec],
        out_shape=[jax.ShapeDtypeStruct((m, D_MODEL), F32),
                   jax.ShapeDtypeStruct((m, D_MODEL), BF16)],
        compiler_params=_params(("arbitrary",)),
        name="out_proj",
    )(attn, gm, w_out_bf16, x2d, gain)


def _mlp_kernel(x_ref, xn_ref, wu_ref, wd_ref, g_ref, *out_refs, last_layer):
    f = pl.program_id(1)
    acc_ref = out_refs[0]

    @pl.when(f == 0)
    def _():
        acc_ref[...] = x_ref[...]

    h = jnp.dot(xn_ref[...], wu_ref[...], preferred_element_type=F32)
    h = jnp.square(jnp.maximum(h, 0.0)).astype(BF16)
    acc_ref[...] += jnp.dot(h, wd_ref[...], preferred_element_type=F32)

    @pl.when(f == pl.num_programs(1) - 1)
    def _():
        normed = _rms_normalize(acc_ref[...], g_ref[...])
        if last_layer:
            acc_ref[...] = normed
        else:
            out_refs[1][...] = normed.astype(BF16)


def _mlp(x1, xn, w_up_bf16, w_down_bf16, layer, next_gain, last_layer):
    m = x1.shape[0]
    tm, tf = MLP_TM, MLP_TF
    row_spec = pl.BlockSpec((tm, D_MODEL), lambda i, f: (i, 0))
    out_specs = [row_spec]
    out_shape = [jax.ShapeDtypeStruct((m, D_MODEL), F32)]
    if not last_layer:
        out_specs.append(row_spec)
        out_shape.append(jax.ShapeDtypeStruct((m, D_MODEL), BF16))
    return pl.pallas_call(
        functools.partial(_mlp_kernel, last_layer=last_layer),
        grid=(m // tm, D_FF // tf),
        in_specs=[
            row_spec,
            row_spec,
            pl.BlockSpec((None, D_MODEL, tf), lambda i, f: (layer, 0, f)),
            pl.BlockSpec((None, tf, D_MODEL), lambda i, f: (layer, f, 0)),
            pl.BlockSpec((1, D_MODEL), lambda i, f: (0, 0)),
        ],
        out_specs=out_specs,
        out_shape=out_shape,
        compiler_params=_params(("parallel", "arbitrary")),
        name="mlp",
    )(x1, xn, w_up_bf16, w_down_bf16, next_gain)


def _rotary_tables(seq):
    half = HEAD_DIM // 2
    inv_freq = ROPE_THETA ** (-jnp.arange(half, dtype=F32) / half)
    ang = jnp.arange(seq, dtype=jnp.int32).astype(F32)[:, None] * inv_freq[None, :]
    cos, sin = jnp.cos(ang), jnp.sin(ang)
    return jnp.concatenate([cos, cos], axis=-1), jnp.concatenate([-sin, sin], axis=-1)


def kernel(x, norm1_g, w_in, gmlp_ln_g, gmlp_ln_b, w_spatial, b_spatial, w_out,
           norm2_g, w_up, w_down, final_g):
    batch, seq, _ = x.shape
    depth = w_in.shape[0]
    m = batch * seq
    cos_t, sin_t = _rotary_tables(seq)
    gains1 = norm1_g.reshape(depth, 1, D_MODEL)
    gains2 = norm2_g.reshape(depth, 1, D_MODEL)
    final_gain = final_g.reshape(1, D_MODEL)

    w_in_bf16, w_out_bf16 = w_in.astype(BF16), w_out.astype(BF16)
    w_up_bf16, w_down_bf16 = w_up.astype(BF16), w_down.astype(BF16)

    x2d = x.reshape(m, D_MODEL)
    xn = None
    for l in range(depth):
        last_layer = l == depth - 1
        gmlp_args = (w_in_bf16, l, gmlp_ln_g[l].reshape(1, GMLP_WIDTH),
                     gmlp_ln_b[l].reshape(1, GMLP_WIDTH), w_spatial[l], jnp.transpose(b_spatial[l]))
        if xn is None:
            gm, xn = _gmlp(x2d, gains1[l], *gmlp_args)
        else:
            gm, = _gmlp(xn, None, *gmlp_args)
        qkv = [a.reshape(batch, 3 * ATTN_HEADS, seq, HEAD_DIM)
               for a in _qkv_proj(xn, w_in_bf16, l, cos_t, sin_t, batch, seq)]
        attn = _attention(qkv, batch, seq).reshape(m, ATTN_WIDTH)
        x1, xn2 = _out_proj(attn, gm, w_out_bf16, l, x2d, gains2[l])
        next_gain = final_gain if last_layer else gains1[l + 1]
        res = _mlp(x1, xn2, w_up_bf16, w_down_bf16, l, next_gain, last_layer)
        if last_layer:
            x2d = res[0]
        else:
            x2d, xn = res
    return x2d.reshape(batch, seq, D_MODEL)
```

```python
import functools
import math

import jax
import jax.numpy as jnp
from jax import lax
from jax.experimental import pallas as pl
from jax.experimental.pallas import tpu as pltpu

D_MODEL = 2048
HEAD_DIM = 128
ATTN_HEADS = 8
ATTN_WIDTH = ATTN_HEADS * HEAD_DIM
GMLP_GROUPS = 8
GMLP_GROUP_DIM = 128
GMLP_WIDTH = GMLP_GROUPS * GMLP_GROUP_DIM
CHUNK = 128
ATTN_BLOCK = 128
DILATIONS = (1, 4, 16)
ROPE_THETA = 10000.0
D_FF = 4 * D_MODEL
NORM_EPS = 1e-6
LN_EPS = 1e-5

V7X_VMEM_LIMIT_BYTES = 56 * 1024 * 1024

PROJ_TM = 512
GMLP_TM = 512
OUT_TM = 512
MLP_TM = 512
MLP_TF = 1024
HOP_ROWS = 512
MAX_SINGLE_ACCESS_STRIDE = 4
STAGING_SLOT = len(DILATIONS) - 1

F32 = jnp.float32
BF16 = jnp.bfloat16


def _rms_normalize(x, gain):
    return x * lax.rsqrt(jnp.mean(x * x, axis=-1, keepdims=True) + NORM_EPS) * gain


def _params(semantics):
    return pltpu.CompilerParams(dimension_semantics=semantics,
                                vmem_limit_bytes=V7X_VMEM_LIMIT_BYTES)


def _resident(shape, index_map):
    return pl.BlockSpec(shape, index_map, pipeline_mode=pl.Buffered(1))


def _cast_rider(stacked_f32, layer, n_steps, step_of):
    _, rows, cols = stacked_f32.shape
    chunk = rows // n_steps
    in_spec = pl.BlockSpec((None, chunk, cols), lambda *ids: (layer, step_of(*ids), 0))
    out_spec = pl.BlockSpec((chunk, cols), lambda *ids: (step_of(*ids), 0))
    return stacked_f32, in_spec, out_spec, jax.ShapeDtypeStruct((rows, cols), BF16)


def _run_cast_riders(src_refs, dst_refs):
    for src, dst in zip(src_refs, dst_refs):
        dst[...] = src[...].astype(BF16)


def _qkv_proj_kernel(xn_ref, wq_ref, wk_ref, wv_ref, cos_ref, sin_ref,
                     nat_ref, d4_ref, d16_ref, *stage_refs):
    tm = xn_ref.shape[0]
    quarter = tm // 4
    xn = xn_ref[...]
    jobs = ((wq_ref, True), (wk_ref, True), (wv_ref, False))
    for t, (w_ref, rotate) in enumerate(jobs):
        head_ref, head4_ref = stage_refs[2 * t], stage_refs[2 * t + 1]
        z = jnp.dot(xn, w_ref[...], preferred_element_type=F32)
        for h in range(ATTN_HEADS):
            y = z[:, h * HEAD_DIM:(h + 1) * HEAD_DIM]
            if rotate:
                y = y * cos_ref[...] + pltpu.roll(y, HEAD_DIM // 2, 1) * sin_ref[...]
            slab = t * ATTN_HEADS + h
            nat_ref[slab] = y.astype(BF16)
            head_ref[h] = y
            for r4 in range(4):
                y4 = head_ref[h, pl.ds(r4, quarter, stride=4), :]
                d4_ref[slab, r4] = y4.astype(BF16)
                head4_ref[h, r4 * quarter:(r4 + 1) * quarter, :] = y4
            for r16 in range(16):
                c, r4 = divmod(r16, 4)
                y16 = head4_ref[h, pl.ds(r4 * quarter + c, tm // 16, stride=4), :]
                d16_ref[slab, r16] = y16.astype(BF16)


def _qkv_proj(xn, w_in_bf16, cos_t, sin_t, batch, seq):
    m = xn.shape[0]
    tm = PROJ_TM
    nt = seq // tm
    slabs = 3 * ATTN_HEADS

    def dil_shape(d):
        return jax.ShapeDtypeStruct((batch, slabs, d, seq // d, HEAD_DIM), BF16)

    def dil_spec(d):
        return pl.BlockSpec((None, slabs, d, tm // d, HEAD_DIM),
                            lambda i: (i // nt, 0, 0, i % nt, 0))

    def w_spec(col_block):
        return _resident((D_MODEL, ATTN_WIDTH), lambda i: (0, col_block))

    table_spec = pl.BlockSpec((tm, HEAD_DIM), lambda i: (i % nt, 0))
    head_scratch = pltpu.VMEM((ATTN_HEADS, tm, HEAD_DIM), F32)
    return pl.pallas_call(
        _qkv_proj_kernel,
        grid=(m // tm,),
        in_specs=[pl.BlockSpec((tm, D_MODEL), lambda i: (i, 0)),
                  w_spec(0), w_spec(1), w_spec(2), table_spec, table_spec],
        out_specs=[pl.BlockSpec((None, slabs, tm, HEAD_DIM), lambda i: (i // nt, 0, i % nt, 0)),
                   dil_spec(4), dil_spec(16)],
        out_shape=[jax.ShapeDtypeStruct((batch, slabs, seq, HEAD_DIM), BF16),
                   dil_shape(4), dil_shape(16)],
        scratch_shapes=[head_scratch] * 6,
        compiler_params=_params(("arbitrary",)),
        name="qkv_proj",
    )(xn, w_in_bf16, w_in_bf16, w_in_bf16, cos_t, sin_t)


def _gmlp_body(xn, wu_ref, wv_ref, lng_ref, lnb_ref, ws_ref, bs_ref, out_ref):
    tm = xn.shape[0]
    a = jax.nn.gelu(jnp.dot(xn, wv_ref[...], preferred_element_type=F32))
    u = jax.nn.gelu(jnp.dot(xn, wu_ref[...], preferred_element_type=F32))
    row = lax.broadcasted_iota(jnp.int32, (CHUNK, CHUNK), 0)
    col = lax.broadcasted_iota(jnp.int32, (CHUNK, CHUNK), 1)
    causal = col <= row
    for g in range(GMLP_GROUPS):
        cols = slice(g * GMLP_GROUP_DIM, (g + 1) * GMLP_GROUP_DIM)
        ag = a[:, cols]
        mu = jnp.mean(ag, axis=-1, keepdims=True)
        xc = ag - mu
        var = jnp.mean(xc * xc, axis=-1, keepdims=True)
        vn = (xc * lax.rsqrt(var + LN_EPS) * lng_ref[:, cols] + lnb_ref[:, cols]).astype(BF16)
        w = jnp.where(causal, ws_ref[g], 0.0).astype(BF16)
        bias = bs_ref[:, g:g + 1]
        for c in range(tm // CHUNK):
            rows = slice(c * CHUNK, (c + 1) * CHUNK)
            sp = jnp.dot(w, vn[rows], preferred_element_type=F32) + bias
            out_ref[rows, cols] = (u[rows, cols] * sp).astype(out_ref.dtype)


def _gmlp_kernel(*refs, fuse_norm, n_casts):
    n_in = (2 if fuse_norm else 1) + 6
    ins, cast_srcs, outs = refs[:n_in], refs[n_in:n_in + n_casts], refs[n_in + n_casts:]
    cast_dsts = outs[len(outs) - n_casts:]
    if fuse_norm:
        x_ref, g_ref, *body_refs = ins
        xn = _rms_normalize(x_ref[...], g_ref[...]).astype(BF16)
        outs[1][...] = xn
    else:
        xn_ref, *body_refs = ins
        xn = xn_ref[...]
    _gmlp_body(xn, *body_refs, outs[0])
    _run_cast_riders(cast_srcs, cast_dsts)


def _gmlp(x_or_xn, gain, w_in_bf16, ln_g, ln_b, w_spatial, b_spatial_t, casts):
    m = x_or_xn.shape[0]
    tm = GMLP_TM
    n_steps = m // tm
    first_block = 3 * ATTN_WIDTH // GMLP_WIDTH
    row_spec = pl.BlockSpec((tm, D_MODEL), lambda i: (i, 0))
    vec_spec = pl.BlockSpec((1, GMLP_WIDTH), lambda i: (0, 0))
    in_specs = [row_spec]
    operands = [x_or_xn]
    out_specs = [pl.BlockSpec((tm, GMLP_WIDTH), lambda i: (i, 0))]
    out_shape = [jax.ShapeDtypeStruct((m, GMLP_WIDTH), BF16)]
    if gain is not None:
        in_specs.append(pl.BlockSpec((1, D_MODEL), lambda i: (0, 0)))
        operands.append(gain)
        out_specs.append(row_spec)
        out_shape.append(jax.ShapeDtypeStruct((m, D_MODEL), BF16))
    in_specs += [_resident((D_MODEL, GMLP_WIDTH), lambda i: (0, first_block)),
                 _resident((D_MODEL, GMLP_WIDTH), lambda i: (0, first_block + 1)),
                 vec_spec, vec_spec,
                 pl.BlockSpec((GMLP_GROUPS, CHUNK, CHUNK), lambda i: (0, 0, 0)),
                 pl.BlockSpec((CHUNK, GMLP_GROUPS), lambda i: (0, 0))]
    operands += [w_in_bf16, w_in_bf16, ln_g, ln_b, w_spatial, b_spatial_t]
    for weight, layer in casts:
        operand, in_spec, out_spec, shape = _cast_rider(weight, layer, n_steps, lambda i: i)
        operands.append(operand)
        in_specs.append(in_spec)
        out_specs.append(out_spec)
        out_shape.append(shape)
    return pl.pallas_call(
        functools.partial(_gmlp_kernel, fuse_norm=gain is not None, n_casts=len(casts)),
        grid=(n_steps,),
        in_specs=in_specs,
        out_specs=out_specs,
        out_shape=out_shape,
        compiler_params=_params(("arbitrary",)),
        name="gmlp",
    )(*operands)


def _attention_kernel(*refs, n_casts):
    (q1_ref, k1_ref, v1_ref, q4_ref, k4_ref, v4_ref, q16_ref, k16_ref, v16_ref) = refs[:9]
    cast_srcs, out_ref = refs[9:9 + n_casts], refs[9 + n_casts]
    cast_dsts = refs[10 + n_casts:10 + 2 * n_casts]
    acc_ref, max_ref, sum_ref = refs[10 + 2 * n_casts:]
    _run_cast_riders(cast_srcs, cast_dsts)
    seq = out_ref.shape[0]
    blk = ATTN_BLOCK
    exp2_scale = HEAD_DIM ** -0.5 * math.log2(math.e)
    row = lax.broadcasted_iota(jnp.int32, (blk, blk), 0)
    col = lax.broadcasted_iota(jnp.int32, (blk, blk), 1)
    cur_mask = col <= row
    prev_mask = col >= row
    nt_dims = (((1,), (1,)), ((), ()))
    ones = jnp.ones((2 * blk, HEAD_DIM), BF16)

    def attend(slot, d, q_ref, k_ref, v_ref, idx):
        nb = seq // (d * blk)
        n, r = idx % nb, idx // nb
        q = q_ref[pl.ds(idx * blk, blk), :]
        if n == 0:
            n_keys = blk
            keys = pl.ds(idx * blk, n_keys)
            s = lax.dot_general(q, k_ref[keys, :], nt_dims, preferred_element_type=F32)
            s = jnp.where(cur_mask, s, -jnp.inf)
            m = jnp.max(s, axis=-1, keepdims=True)
            p = jnp.exp2((s - m) * exp2_scale).astype(BF16)
        else:
            n_keys = 2 * blk
            keys = pl.ds((idx - 1) * blk, n_keys)
            s = lax.dot_general(q, k_ref[keys, :], nt_dims, preferred_element_type=F32)
            s_p = jnp.where(prev_mask, s[:, :blk], -jnp.inf)
            s_c = jnp.where(cur_mask, s[:, blk:], -jnp.inf)
            m = jnp.max(jnp.maximum(s_c, s_p), axis=-1, keepdims=True)
            p = jnp.concatenate([jnp.exp2((s_p - m) * exp2_scale).astype(BF16),
                                 jnp.exp2((s_c - m) * exp2_scale).astype(BF16)], axis=1)
        v_ext = jnp.concatenate([v_ref[keys, :], ones[:n_keys]], axis=1)
        acc = jnp.dot(p, v_ext, preferred_element_type=F32)
        a0, l0, m0 = acc[:, :HEAD_DIM], acc[:, HEAD_DIM:], jnp.broadcast_to(m, (blk, HEAD_DIM))
        if d == 1:
            rows = pl.ds(n * blk, blk)
            m1, m2 = max_ref[0, rows, :], max_ref[1, rows, :]
            top = jnp.maximum(jnp.maximum(m0, m1), m2)
            e0 = jnp.exp2((m0 - top) * exp2_scale)
            e1 = jnp.exp2((m1 - top) * exp2_scale)
            e2 = jnp.exp2((m2 - top) * exp2_scale)
            den = e0 * l0 + e1 * sum_ref[0, rows, :] + e2 * sum_ref[1, rows, :]
            num = e0 * a0 + e1 * acc_ref[0, rows, :] + e2 * acc_ref[1, rows, :]
            out_ref[rows, :] = (num / den).astype(out_ref.dtype)
            return
        if d <= MAX_SINGLE_ACCESS_STRIDE:
            rows = pl.ds(n * (blk * d) + r, blk, stride=d)
        else:
            slot = STAGING_SLOT
            rows = pl.ds((r % 4) * (seq // 4) + (n * blk * d + r) // 4, blk, stride=d // 4)
        acc_ref[slot, rows, :] = a0
        sum_ref[slot, rows, :] = l0
        max_ref[slot, rows, :] = m0

    dilated = ((0, 4, q4_ref, k4_ref, v4_ref), (1, 16, q16_ref, k16_ref, v16_ref))
    for slot, d, q_ref, k_ref, v_ref in dilated:
        for idx in range(seq // blk):
            attend(slot, d, q_ref, k_ref, v_ref, idx)
        if d > MAX_SINGLE_ACCESS_STRIDE:
            for ref in (acc_ref, sum_ref, max_ref):
                for r4 in range(4):
                    for part in range(seq // (4 * HOP_ROWS)):
                        u0 = part * HOP_ROWS
                        src = pl.ds(r4 * (seq // 4) + u0, HOP_ROWS)
                        ref[slot, pl.ds(4 * u0 + r4, HOP_ROWS, stride=4), :] = ref[STAGING_SLOT, src, :]
    for idx in range(seq // blk):
        attend(None, 1, q1_ref, k1_ref, v1_ref, idx)


def _attention(qkv_by_order, batch, seq, casts):
    def spec(t):
        return pl.BlockSpec((None, None, seq, HEAD_DIM),
                            lambda b, h: (b, t * ATTN_HEADS + h, 0, 0))

    in_specs = [spec(0), spec(1), spec(2)] * len(DILATIONS)
    operands = [a for a in qkv_by_order for _ in range(3)]
    out_specs = [pl.BlockSpec((None, seq, HEAD_DIM), lambda b, h: (b, 0, h))]
    out_shape = [jax.ShapeDtypeStruct((batch, seq, ATTN_WIDTH), BF16)]
    for weight, layer in casts:
        operand, in_spec, out_spec, shape = _cast_rider(
            weight, layer, batch * ATTN_HEADS, lambda b, h: b * ATTN_HEADS + h)
        operands.append(operand)
        in_specs.append(in_spec)
        out_specs.append(out_spec)
        out_shape.append(shape)
    stat = pltpu.VMEM((STAGING_SLOT + 1, seq, HEAD_DIM), F32)
    return pl.pallas_call(
        functools.partial(_attention_kernel, n_casts=len(casts)),
        grid=(batch, ATTN_HEADS),
        in_specs=in_specs,
        out_specs=out_specs,
        out_shape=out_shape,
        scratch_shapes=[stat, stat, stat],
        compiler_params=_params(("arbitrary", "arbitrary")),
        name="dilated_attention",
    )(*operands)


def _out_proj_kernel(attn_ref, gm_ref, w_ref, x_ref, g_ref, x1_ref, xn_ref):
    mix = jnp.concatenate([attn_ref[...], gm_ref[...]], axis=1)
    x1 = x_ref[...] + jnp.dot(mix, w_ref[...], preferred_element_type=F32)
    x1_ref[...] = x1
    xn_ref[...] = _rms_normalize(x1, g_ref[...]).astype(BF16)


def _out_proj(attn, gm, w_out_bf16, x2d, gain):
    m = x2d.shape[0]
    tm = OUT_TM
    row_spec = pl.BlockSpec((tm, D_MODEL), lambda i: (i, 0))
    return pl.pallas_call(
        _out_proj_kernel,
        grid=(m // tm,),
        in_specs=[
            pl.BlockSpec((tm, ATTN_WIDTH), lambda i: (i, 0)),
            pl.BlockSpec((tm, GMLP_WIDTH), lambda i: (i, 0)),
            _resident((ATTN_WIDTH + GMLP_WIDTH, D_MODEL), lambda i: (0, 0)),
            row_spec,
            pl.BlockSpec((1, D_MODEL), lambda i: (0, 0)),
        ],
        out_specs=[row_spec, row_spec],
        out_shape=[jax.ShapeDtypeStruct((m, D_MODEL), F32),
                   jax.ShapeDtypeStruct((m, D_MODEL), BF16)],
        compiler_params=_params(("arbitrary",)),
        name="out_proj",
    )(attn, gm, w_out_bf16, x2d, gain)


def _mlp_kernel(x_ref, xn_ref, wu_ref, wd_ref, g_ref, *out_refs, last_layer):
    f = pl.program_id(1)
    acc_ref = out_refs[0]

    @pl.when(f == 0)
    def _():
        acc_ref[...] = x_ref[...]

    h = jnp.dot(xn_ref[...], wu_ref[...], preferred_element_type=F32)
    h = jnp.square(jnp.maximum(h, 0.0)).astype(BF16)
    acc_ref[...] += jnp.dot(h, wd_ref[...], preferred_element_type=F32)

    @pl.when(f == pl.num_programs(1) - 1)
    def _():
        normed = _rms_normalize(acc_ref[...], g_ref[...])
        if last_layer:
            acc_ref[...] = normed
        else:
            out_refs[1][...] = normed.astype(BF16)


def _mlp(x1, xn, w_up_bf16, w_down_bf16, next_gain, last_layer):
    m = x1.shape[0]
    tm, tf = MLP_TM, MLP_TF
    row_spec = pl.BlockSpec((tm, D_MODEL), lambda i, f: (i, 0))
    out_specs = [row_spec]
    out_shape = [jax.ShapeDtypeStruct((m, D_MODEL), F32)]
    if not last_layer:
        out_specs.append(row_spec)
        out_shape.append(jax.ShapeDtypeStruct((m, D_MODEL), BF16))
    return pl.pallas_call(
        functools.partial(_mlp_kernel, last_layer=last_layer),
        grid=(m // tm, D_FF // tf),
        in_specs=[
            row_spec,
            row_spec,
            pl.BlockSpec((D_MODEL, tf), lambda i, f: (0, f)),
            pl.BlockSpec((tf, D_MODEL), lambda i, f: (f, 0)),
            pl.BlockSpec((1, D_MODEL), lambda i, f: (0, 0)),
        ],
        out_specs=out_specs,
        out_shape=out_shape,
        compiler_params=_params(("arbitrary", "arbitrary")),
        name="mlp",
    )(x1, xn, w_up_bf16, w_down_bf16, next_gain)


def _rotary_tables(seq):
    half = HEAD_DIM // 2
    inv_freq = ROPE_THETA ** (-jnp.arange(half, dtype=F32) / half)
    ang = jnp.arange(seq, dtype=jnp.int32).astype(F32)[:, None] * inv_freq[None, :]
    cos, sin = jnp.cos(ang), jnp.sin(ang)
    return jnp.concatenate([cos, cos], axis=-1), jnp.concatenate([-sin, sin], axis=-1)


def kernel(x, norm1_g, w_in, gmlp_ln_g, gmlp_ln_b, w_spatial, b_spatial, w_out,
           norm2_g, w_up, w_down, final_g):
    batch, seq, _ = x.shape
    depth = w_in.shape[0]
    m = batch * seq
    cos_t, sin_t = _rotary_tables(seq)
    gains1 = norm1_g.reshape(depth, 1, D_MODEL)
    gains2 = norm2_g.reshape(depth, 1, D_MODEL)
    final_gain = final_g.reshape(1, D_MODEL)

    w_in_bf16 = w_in[0].astype(BF16)

    x2d = x.reshape(m, D_MODEL)
    xn = None
    for l in range(depth):
        last_layer = l == depth - 1
        gmlp_casts = [(w_up, l)] + ([] if last_layer else [(w_in, l + 1)])
        gm, *rest = _gmlp(x2d if xn is None else xn, gains1[l] if xn is None else None,
                          w_in_bf16, gmlp_ln_g[l].reshape(1, GMLP_WIDTH),
                          gmlp_ln_b[l].reshape(1, GMLP_WIDTH), w_spatial[l],
                          jnp.transpose(b_spatial[l]), gmlp_casts)
        if xn is None:
            xn, *rest = rest
        w_up_bf16, *w_in_next = rest
        qkv = [a.reshape(batch, 3 * ATTN_HEADS, seq, HEAD_DIM)
               for a in _qkv_proj(xn, w_in_bf16, cos_t, sin_t, batch, seq)]
        attn, w_down_bf16, w_out_bf16 = _attention(qkv, batch, seq, [(w_down, l), (w_out, l)])
        x1, xn2 = _out_proj(attn.reshape(m, ATTN_WIDTH), gm, w_out_bf16, x2d, gains2[l])
        next_gain = final_gain if last_layer else gains1[l + 1]
        res = _mlp(x1, xn2, w_up_bf16, w_down_bf16, next_gain, last_layer)
        if last_layer:
            x2d = res[0]
        else:
            x2d, xn = res
            w_in_bf16, = w_in_next
    return x2d.reshape(batch, seq, D_MODEL)
```

```python
import functools
import math

import jax
import jax.numpy as jnp
from jax import lax
from jax.experimental import pallas as pl
from jax.experimental.pallas import tpu as pltpu

D_MODEL = 2048
HEAD_DIM = 128
ATTN_HEADS = 8
ATTN_WIDTH = ATTN_HEADS * HEAD_DIM
GMLP_GROUPS = 8
GMLP_GROUP_DIM = 128
GMLP_WIDTH = GMLP_GROUPS * GMLP_GROUP_DIM
CHUNK = 128
ATTN_BLOCK = 128
DILATIONS = (1, 4, 16)
ROPE_THETA = 10000.0
D_FF = 4 * D_MODEL
NORM_EPS = 1e-6
LN_EPS = 1e-5

V7X_VMEM_LIMIT_BYTES = 56 * 1024 * 1024

PROJ_TM = 512
GMLP_TM = 512
OUT_TM = 512
MLP_TM = 512
MLP_TF = 1024
HOP_ROWS = 512
MAX_SINGLE_ACCESS_STRIDE = 4
STAGING_SLOT = len(DILATIONS) - 1

F32 = jnp.float32
BF16 = jnp.bfloat16


def _rms_normalize(x, gain):
    return x * lax.rsqrt(jnp.mean(x * x, axis=-1, keepdims=True) + NORM_EPS) * gain


def _params(semantics):
    return pltpu.CompilerParams(dimension_semantics=semantics,
                                vmem_limit_bytes=V7X_VMEM_LIMIT_BYTES)


def _resident(shape, index_map):
    return pl.BlockSpec(shape, index_map, pipeline_mode=pl.Buffered(1))


def _cast_rider(stacked_f32, layer, n_steps, step_of):
    _, rows, cols = stacked_f32.shape
    chunk = rows // n_steps
    in_spec = pl.BlockSpec((None, chunk, cols), lambda *ids: (layer, step_of(*ids), 0))
    out_spec = pl.BlockSpec((chunk, cols), lambda *ids: (step_of(*ids), 0))
    return stacked_f32, in_spec, out_spec, jax.ShapeDtypeStruct((rows, cols), BF16)


def _run_cast_riders(src_refs, dst_refs):
    for src, dst in zip(src_refs, dst_refs):
        dst[...] = src[...].astype(BF16)


def _qkv_proj_kernel(xn_ref, wq_ref, wk_ref, wv_ref, cos_ref, sin_ref,
                     nat_ref, d4_ref, d16_ref, *stage_refs):
    tm = xn_ref.shape[0]
    quarter = tm // 4
    xn = xn_ref[...]
    jobs = ((wq_ref, True), (wk_ref, True), (wv_ref, False))
    for t, (w_ref, rotate) in enumerate(jobs):
        head_ref, head4_ref = stage_refs[2 * t], stage_refs[2 * t + 1]
        z = jnp.dot(xn, w_ref[...], preferred_element_type=F32)
        for h in range(ATTN_HEADS):
            y = z[:, h * HEAD_DIM:(h + 1) * HEAD_DIM]
            if rotate:
                y = y * cos_ref[...] + pltpu.roll(y, HEAD_DIM // 2, 1) * sin_ref[...]
            slab = t * ATTN_HEADS + h
            nat_ref[slab] = y.astype(BF16)
            head_ref[h] = y
            for r4 in range(4):
                y4 = head_ref[h, pl.ds(r4, quarter, stride=4), :]
                d4_ref[slab, r4] = y4.astype(BF16)
                head4_ref[h, r4 * quarter:(r4 + 1) * quarter, :] = y4
            for r16 in range(16):
                c, r4 = divmod(r16, 4)
                y16 = head4_ref[h, pl.ds(r4 * quarter + c, tm // 16, stride=4), :]
                d16_ref[slab, r16] = y16.astype(BF16)


def _qkv_proj(xn, w_in_bf16, cos_t, sin_t, batch, seq):
    m = xn.shape[0]
    tm = PROJ_TM
    nt = seq // tm
    slabs = 3 * ATTN_HEADS

    def dil_shape(d):
        return jax.ShapeDtypeStruct((batch, slabs, d, seq // d, HEAD_DIM), BF16)

    def dil_spec(d):
        return pl.BlockSpec((None, slabs, d, tm // d, HEAD_DIM),
                            lambda i: (i // nt, 0, 0, i % nt, 0))

    def w_spec(col_block):
        return _resident((D_MODEL, ATTN_WIDTH), lambda i: (0, col_block))

    table_spec = pl.BlockSpec((tm, HEAD_DIM), lambda i: (i % nt, 0))
    head_scratch = pltpu.VMEM((ATTN_HEADS, tm, HEAD_DIM), F32)
    return pl.pallas_call(
        _qkv_proj_kernel,
        grid=(m // tm,),
        in_specs=[pl.BlockSpec((tm, D_MODEL), lambda i: (i, 0)),
                  w_spec(0), w_spec(1), w_spec(2), table_spec, table_spec],
        out_specs=[pl.BlockSpec((None, slabs, tm, HEAD_DIM), lambda i: (i // nt, 0, i % nt, 0)),
                   dil_spec(4), dil_spec(16)],
        out_shape=[jax.ShapeDtypeStruct((batch, slabs, seq, HEAD_DIM), BF16),
                   dil_shape(4), dil_shape(16)],
        scratch_shapes=[head_scratch] * 6,
        compiler_params=_params(("arbitrary",)),
        name="qkv_proj",
    )(xn, w_in_bf16, w_in_bf16, w_in_bf16, cos_t, sin_t)


def _gmlp_body(xn, wu_ref, wv_ref, lng_ref, lnb_ref, ws_ref, bs_ref, out_ref):
    tm = xn.shape[0]
    a = jax.nn.gelu(jnp.dot(xn, wv_ref[...], preferred_element_type=F32))
    u = jax.nn.gelu(jnp.dot(xn, wu_ref[...], preferred_element_type=F32))
    row = lax.broadcasted_iota(jnp.int32, (CHUNK, CHUNK), 0)
    col = lax.broadcasted_iota(jnp.int32, (CHUNK, CHUNK), 1)
    causal = col <= row
    for g in range(GMLP_GROUPS):
        cols = slice(g * GMLP_GROUP_DIM, (g + 1) * GMLP_GROUP_DIM)
        ag = a[:, cols]
        mu = jnp.mean(ag, axis=-1, keepdims=True)
        xc = ag - mu
        var = jnp.mean(xc * xc, axis=-1, keepdims=True)
        vn = (xc * lax.rsqrt(var + LN_EPS) * lng_ref[:, cols] + lnb_ref[:, cols]).astype(BF16)
        w = jnp.where(causal, ws_ref[g], 0.0).astype(BF16)
        bias = bs_ref[:, g:g + 1]
        for c in range(tm // CHUNK):
            rows = slice(c * CHUNK, (c + 1) * CHUNK)
            sp = jnp.dot(w, vn[rows], preferred_element_type=F32) + bias
            out_ref[rows, cols] = (u[rows, cols] * sp).astype(out_ref.dtype)


def _gmlp_kernel(*refs, fuse_norm, n_casts):
    n_in = (2 if fuse_norm else 1) + 6
    ins, cast_srcs, outs = refs[:n_in], refs[n_in:n_in + n_casts], refs[n_in + n_casts:]
    cast_dsts = outs[len(outs) - n_casts:]
    if fuse_norm:
        x_ref, g_ref, *body_refs = ins
        xn = _rms_normalize(x_ref[...], g_ref[...]).astype(BF16)
        outs[1][...] = xn
    else:
        xn_ref, *body_refs = ins
        xn = xn_ref[...]
    _gmlp_body(xn, *body_refs, outs[0])
    _run_cast_riders(cast_srcs, cast_dsts)


def _gmlp(x_or_xn, gain, w_in_bf16, ln_g, ln_b, w_spatial, b_spatial_t, casts):
    m = x_or_xn.shape[0]
    tm = GMLP_TM
    n_steps = m // tm
    first_block = 3 * ATTN_WIDTH // GMLP_WIDTH
    row_spec = pl.BlockSpec((tm, D_MODEL), lambda i: (i, 0))
    vec_spec = pl.BlockSpec((1, GMLP_WIDTH), lambda i: (0, 0))
    in_specs = [row_spec]
    operands = [x_or_xn]
    out_specs = [pl.BlockSpec((tm, GMLP_WIDTH), lambda i: (i, 0))]
    out_shape = [jax.ShapeDtypeStruct((m, GMLP_WIDTH), BF16)]
    if gain is not None:
        in_specs.append(pl.BlockSpec((1, D_MODEL), lambda i: (0, 0)))
        operands.append(gain)
        out_specs.append(row_spec)
        out_shape.append(jax.ShapeDtypeStruct((m, D_MODEL), BF16))
    in_specs += [pl.BlockSpec((D_MODEL, GMLP_WIDTH), lambda i: (0, first_block)),
                 pl.BlockSpec((D_MODEL, GMLP_WIDTH), lambda i: (0, first_block + 1)),
                 vec_spec, vec_spec,
                 pl.BlockSpec((GMLP_GROUPS, CHUNK, CHUNK), lambda i: (0, 0, 0)),
                 pl.BlockSpec((CHUNK, GMLP_GROUPS), lambda i: (0, 0))]
    operands += [w_in_bf16, w_in_bf16, ln_g, ln_b, w_spatial, b_spatial_t]
    for weight, layer in casts:
        operand, in_spec, out_spec, shape = _cast_rider(weight, layer, n_steps, lambda i: i)
        operands.append(operand)
        in_specs.append(in_spec)
        out_specs.append(out_spec)
        out_shape.append(shape)
    return pl.pallas_call(
        functools.partial(_gmlp_kernel, fuse_norm=gain is not None, n_casts=len(casts)),
        grid=(n_steps,),
        in_specs=in_specs,
        out_specs=out_specs,
        out_shape=out_shape,
        compiler_params=_params(("arbitrary",)),
        name="gmlp",
    )(*operands)


def _attention_kernel(*refs, n_casts):
    (q1_ref, k1_ref, v1_ref, q4_ref, k4_ref, v4_ref, q16_ref, k16_ref, v16_ref) = refs[:9]
    cast_srcs, out_ref = refs[9:9 + n_casts], refs[9 + n_casts]
    cast_dsts = refs[10 + n_casts:10 + 2 * n_casts]
    acc_ref, max_ref, sum_ref = refs[10 + 2 * n_casts:]
    _run_cast_riders(cast_srcs, cast_dsts)
    seq = out_ref.shape[0]
    blk = ATTN_BLOCK
    exp2_scale = HEAD_DIM ** -0.5 * math.log2(math.e)
    row = lax.broadcasted_iota(jnp.int32, (blk, blk), 0)
    col = lax.broadcasted_iota(jnp.int32, (blk, blk), 1)
    cur_mask = col <= row
    prev_mask = col >= row
    nt_dims = (((1,), (1,)), ((), ()))
    ones = jnp.ones((2 * blk, HEAD_DIM), BF16)

    def attend(slot, d, q_ref, k_ref, v_ref, idx):
        nb = seq // (d * blk)
        n, r = idx % nb, idx // nb
        q = q_ref[pl.ds(idx * blk, blk), :]
        if n == 0:
            n_keys = blk
            keys = pl.ds(idx * blk, n_keys)
            s = lax.dot_general(q, k_ref[keys, :], nt_dims, preferred_element_type=F32)
            s = jnp.where(cur_mask, s, -jnp.inf)
            m = jnp.max(s, axis=-1, keepdims=True)
            p = jnp.exp2((s - m) * exp2_scale).astype(BF16)
        else:
            n_keys = 2 * blk
            keys = pl.ds((idx - 1) * blk, n_keys)
            s = lax.dot_general(q, k_ref[keys, :], nt_dims, preferred_element_type=F32)
            s_p = jnp.where(prev_mask, s[:, :blk], -jnp.inf)
            s_c = jnp.where(cur_mask, s[:, blk:], -jnp.inf)
            m = jnp.max(jnp.maximum(s_c, s_p), axis=-1, keepdims=True)
            p = jnp.concatenate([jnp.exp2((s_p - m) * exp2_scale).astype(BF16),
                                 jnp.exp2((s_c - m) * exp2_scale).astype(BF16)], axis=1)
        v_ext = jnp.concatenate([v_ref[keys, :], ones[:n_keys]], axis=1)
        acc = jnp.dot(p, v_ext, preferred_element_type=F32)
        a0, l0, m0 = acc[:, :HEAD_DIM], acc[:, HEAD_DIM:], jnp.broadcast_to(m, (blk, HEAD_DIM))
        if d == 1:
            rows = pl.ds(n * blk, blk)
            m1, m2 = max_ref[0, rows, :], max_ref[1, rows, :]
            top = jnp.maximum(jnp.maximum(m0, m1), m2)
            e0 = jnp.exp2((m0 - top) * exp2_scale)
            e1 = jnp.exp2((m1 - top) * exp2_scale)
            e2 = jnp.exp2((m2 - top) * exp2_scale)
            den = e0 * l0 + e1 * sum_ref[0, rows, :] + e2 * sum_ref[1, rows, :]
            num = e0 * a0 + e1 * acc_ref[0, rows, :] + e2 * acc_ref[1, rows, :]
            out_ref[rows, :] = (num / den).astype(out_ref.dtype)
            return
        if d <= MAX_SINGLE_ACCESS_STRIDE:
            rows = pl.ds(n * (blk * d) + r, blk, stride=d)
        else:
            slot = STAGING_SLOT
            rows = pl.ds((r % 4) * (seq // 4) + (n * blk * d + r) // 4, blk, stride=d // 4)
        acc_ref[slot, rows, :] = a0
        sum_ref[slot, rows, :] = l0
        max_ref[slot, rows, :] = m0

    dilated = ((0, 4, q4_ref, k4_ref, v4_ref), (1, 16, q16_ref, k16_ref, v16_ref))
    for slot, d, q_ref, k_ref, v_ref in dilated:
        for idx in range(seq // blk):
            attend(slot, d, q_ref, k_ref, v_ref, idx)
        if d > MAX_SINGLE_ACCESS_STRIDE:
            for ref in (acc_ref, sum_ref, max_ref):
                for r4 in range(4):
                    for part in range(seq // (4 * HOP_ROWS)):
                        u0 = part * HOP_ROWS
                        src = pl.ds(r4 * (seq // 4) + u0, HOP_ROWS)
                        ref[slot, pl.ds(4 * u0 + r4, HOP_ROWS, stride=4), :] = ref[STAGING_SLOT, src, :]
    for idx in range(seq // blk):
        attend(None, 1, q1_ref, k1_ref, v1_ref, idx)


def _attention(qkv_by_order, batch, seq, casts):
    def spec(t):
        return pl.BlockSpec((None, None, seq, HEAD_DIM),
                            lambda b, h: (b, t * ATTN_HEADS + h, 0, 0))

    in_specs = [spec(0), spec(1), spec(2)] * len(DILATIONS)
    operands = [a for a in qkv_by_order for _ in range(3)]
    out_specs = [pl.BlockSpec((None, seq, HEAD_DIM), lambda b, h: (b, 0, h))]
    out_shape = [jax.ShapeDtypeStruct((batch, seq, ATTN_WIDTH), BF16)]
    for weight, layer in casts:
        operand, in_spec, out_spec, shape = _cast_rider(
            weight, layer, batch * ATTN_HEADS, lambda b, h: b * ATTN_HEADS + h)
        operands.append(operand)
        in_specs.append(in_spec)
        out_specs.append(out_spec)
        out_shape.append(shape)
    stat = pltpu.VMEM((STAGING_SLOT + 1, seq, HEAD_DIM), F32)
    return pl.pallas_call(
        functools.partial(_attention_kernel, n_casts=len(casts)),
        grid=(batch, ATTN_HEADS),
        in_specs=in_specs,
        out_specs=out_specs,
        out_shape=out_shape,
        scratch_shapes=[stat, stat, stat],
        compiler_params=_params(("arbitrary", "arbitrary")),
        name="dilated_attention",
    )(*operands)


def _out_proj_kernel(attn_ref, gm_ref, w_ref, x_ref, g_ref, x1_ref, xn_ref):
    mix = jnp.concatenate([attn_ref[...], gm_ref[...]], axis=1)
    x1 = x_ref[...] + jnp.dot(mix, w_ref[...], preferred_element_type=F32)
    x1_ref[...] = x1
    xn_ref[...] = _rms_normalize(x1, g_ref[...]).astype(BF16)


def _out_proj(attn, gm, w_out_bf16, x2d, gain):
    m = x2d.shape[0]
    tm = OUT_TM
    row_spec = pl.BlockSpec((tm, D_MODEL), lambda i: (i, 0))
    return pl.pallas_call(
        _out_proj_kernel,
        grid=(m // tm,),
        in_specs=[
            pl.BlockSpec((tm, ATTN_WIDTH), lambda i: (i, 0)),
            pl.BlockSpec((tm, GMLP_WIDTH), lambda i: (i, 0)),
            pl.BlockSpec((ATTN_WIDTH + GMLP_WIDTH, D_MODEL), lambda i: (0, 0)),
            row_spec,
            pl.BlockSpec((1, D_MODEL), lambda i: (0, 0)),
        ],
        out_specs=[row_spec, row_spec],
        out_shape=[jax.ShapeDtypeStruct((m, D_MODEL), F32),
                   jax.ShapeDtypeStruct((m, D_MODEL), BF16)],
        compiler_params=_params(("arbitrary",)),
        name="out_proj",
    )(attn, gm, w_out_bf16, x2d, gain)


def _mlp_kernel(x_ref, xn_ref, wu_ref, wd_ref, g_ref, *out_refs, last_layer):
    f = pl.program_id(1)
    acc_ref = out_refs[0]

    @pl.when(f == 0)
    def _():
        acc_ref[...] = x_ref[...]

    h = jnp.dot(xn_ref[...], wu_ref[...], preferred_element_type=F32)
    h = jnp.square(jnp.maximum(h, 0.0)).astype(BF16)
    acc_ref[...] += jnp.dot(h, wd_ref[...], preferred_element_type=F32)

    @pl.when(f == pl.num_programs(1) - 1)
    def _():
        normed = _rms_normalize(acc_ref[...], g_ref[...])
        if last_layer:
            acc_ref[...] = normed
        else:
            out_refs[1][...] = normed.astype(BF16)


def _mlp(x1, xn, w_up_bf16, w_down_bf16, next_gain, last_layer):
    m = x1.shape[0]
    tm, tf = MLP_TM, MLP_TF
    row_spec = pl.BlockSpec((tm, D_MODEL), lambda i, f: (i, 0))
    out_specs = [row_spec]
    out_shape = [jax.ShapeDtypeStruct((m, D_MODEL), F32)]
    if not last_layer:
        out_specs.append(row_spec)
        out_shape.append(jax.ShapeDtypeStruct((m, D_MODEL), BF16))
    return pl.pallas_call(
        functools.partial(_mlp_kernel, last_layer=last_layer),
        grid=(m // tm, D_FF // tf),
        in_specs=[
            row_spec,
            row_spec,
            pl.BlockSpec((D_MODEL, tf), lambda i, f: (0, f)),
            pl.BlockSpec((tf, D_MODEL), lambda i, f: (f, 0)),
            pl.BlockSpec((1, D_MODEL), lambda i, f: (0, 0)),
        ],
        out_specs=out_specs,
        out_shape=out_shape,
        compiler_params=_params(("arbitrary", "arbitrary")),
        name="mlp",
    )(x1, xn, w_up_bf16, w_down_bf16, next_gain)


def _rotary_tables(seq):
    half = HEAD_DIM // 2
    inv_freq = ROPE_THETA ** (-jnp.arange(half, dtype=F32) / half)
    ang = jnp.arange(seq, dtype=jnp.int32).astype(F32)[:, None] * inv_freq[None, :]
    cos, sin = jnp.cos(ang), jnp.sin(ang)
    return jnp.concatenate([cos, cos], axis=-1), jnp.concatenate([-sin, sin], axis=-1)


def kernel(x, norm1_g, w_in, gmlp_ln_g, gmlp_ln_b, w_spatial, b_spatial, w_out,
           norm2_g, w_up, w_down, final_g):
    batch, seq, _ = x.shape
    depth = w_in.shape[0]
    m = batch * seq
    cos_t, sin_t = _rotary_tables(seq)
    gains1 = norm1_g.reshape(depth, 1, D_MODEL)
    gains2 = norm2_g.reshape(depth, 1, D_MODEL)
    final_gain = final_g.reshape(1, D_MODEL)

    w_in_bf16 = w_in[0].astype(BF16)

    x2d = x.reshape(m, D_MODEL)
    xn = None
    for l in range(depth):
        last_layer = l == depth - 1
        gmlp_casts = [(w_up, l)] + ([] if last_layer else [(w_in, l + 1)])
        gm, *rest = _gmlp(x2d if xn is None else xn, gains1[l] if xn is None else None,
                          w_in_bf16, gmlp_ln_g[l].reshape(1, GMLP_WIDTH),
                          gmlp_ln_b[l].reshape(1, GMLP_WIDTH), w_spatial[l],
                          jnp.transpose(b_spatial[l]), gmlp_casts)
        if xn is None:
            xn, *rest = rest
        w_up_bf16, *w_in_next = rest
        qkv = [a.reshape(batch, 3 * ATTN_HEADS, seq, HEAD_DIM)
               for a in _qkv_proj(xn, w_in_bf16, cos_t, sin_t, batch, seq)]
        attn, w_down_bf16, w_out_bf16 = _attention(qkv, batch, seq, [(w_down, l), (w_out, l)])
        x1, xn2 = _out_proj(attn.reshape(m, ATTN_WIDTH), gm, w_out_bf16, x2d, gains2[l])
        next_gain = final_gain if last_layer else gains1[l + 1]
        res = _mlp(x1, xn2, w_up_bf16, w_down_bf16, next_gain, last_layer)
        if last_layer:
            x2d = res[0]
        else:
            x2d, xn = res
            w_in_bf16, = w_in_next
    return x2d.reshape(batch, seq, D_MODEL)
```

```python
import functools
import math

import jax
import jax.numpy as jnp
from jax import lax
from jax.experimental import pallas as pl
from jax.experimental.pallas import tpu as pltpu

D_MODEL = 2048
HEAD_DIM = 128
ATTN_HEADS = 8
ATTN_WIDTH = ATTN_HEADS * HEAD_DIM
GMLP_GROUPS = 8
GMLP_GROUP_DIM = 128
GMLP_WIDTH = GMLP_GROUPS * GMLP_GROUP_DIM
CHUNK = 128
ATTN_BLOCK = 128
DILATIONS = (1, 4, 16)
ROPE_THETA = 10000.0
D_FF = 4 * D_MODEL
NORM_EPS = 1e-6
LN_EPS = 1e-5

V7X_VMEM_LIMIT_BYTES = 56 * 1024 * 1024

PROJ_TM = 512
GMLP_TM = 512
OUT_TM = 512
MLP_TM = 1024
MLP_TF = 512
HOP_ROWS = 512
MAX_SINGLE_ACCESS_STRIDE = 4
STAGING_SLOT = len(DILATIONS) - 1

F32 = jnp.float32
BF16 = jnp.bfloat16


def _rms_normalize(x, gain):
    return x * lax.rsqrt(jnp.mean(x * x, axis=-1, keepdims=True) + NORM_EPS) * gain


def _params(semantics):
    return pltpu.CompilerParams(dimension_semantics=semantics,
                                vmem_limit_bytes=V7X_VMEM_LIMIT_BYTES)


def _resident(shape, index_map):
    return pl.BlockSpec(shape, index_map, pipeline_mode=pl.Buffered(1))


def _cast_rider(stacked_f32, layer, n_steps, step_of):
    _, rows, cols = stacked_f32.shape
    chunk = rows // n_steps
    in_spec = pl.BlockSpec((None, chunk, cols), lambda *ids: (layer, step_of(*ids), 0))
    out_spec = pl.BlockSpec((chunk, cols), lambda *ids: (step_of(*ids), 0))
    return stacked_f32, in_spec, out_spec, jax.ShapeDtypeStruct((rows, cols), BF16)


def _run_cast_riders(src_refs, dst_refs):
    for src, dst in zip(src_refs, dst_refs):
        dst[...] = src[...].astype(BF16)


def _qkv_proj_kernel(xn_ref, wq_ref, wk_ref, wv_ref, cos_ref, sin_ref,
                     nat_ref, d4_ref, d16_ref, *stage_refs):
    tm = xn_ref.shape[0]
    quarter = tm // 4
    xn = xn_ref[...]
    jobs = ((wq_ref, True), (wk_ref, True), (wv_ref, False))
    for t, (w_ref, rotate) in enumerate(jobs):
        head_ref, head4_ref = stage_refs[2 * t], stage_refs[2 * t + 1]
        z = jnp.dot(xn, w_ref[...], preferred_element_type=F32)
        for h in range(ATTN_HEADS):
            y = z[:, h * HEAD_DIM:(h + 1) * HEAD_DIM]
            if rotate:
                y = y * cos_ref[...] + pltpu.roll(y, HEAD_DIM // 2, 1) * sin_ref[...]
            slab = t * ATTN_HEADS + h
            nat_ref[slab] = y.astype(BF16)
            head_ref[h] = y
            for r4 in range(4):
                y4 = head_ref[h, pl.ds(r4, quarter, stride=4), :]
                d4_ref[slab, r4] = y4.astype(BF16)
                head4_ref[h, r4 * quarter:(r4 + 1) * quarter, :] = y4
            for r16 in range(16):
                c, r4 = divmod(r16, 4)
                y16 = head4_ref[h, pl.ds(r4 * quarter + c, tm // 16, stride=4), :]
                d16_ref[slab, r16] = y16.astype(BF16)


def _qkv_proj(xn, w_in_bf16, cos_t, sin_t, batch, seq):
    m = xn.shape[0]
    tm = PROJ_TM
    nt = seq // tm
    slabs = 3 * ATTN_HEADS

    def dil_shape(d):
        return jax.ShapeDtypeStruct((batch, slabs, d, seq // d, HEAD_DIM), BF16)

    def dil_spec(d):
        return pl.BlockSpec((None, slabs, d, tm // d, HEAD_DIM),
                            lambda i: (i // nt, 0, 0, i % nt, 0))

    def w_spec(col_block):
        return _resident((D_MODEL, ATTN_WIDTH), lambda i: (0, col_block))

    table_spec = pl.BlockSpec((tm, HEAD_DIM), lambda i: (i % nt, 0))
    head_scratch = pltpu.VMEM((ATTN_HEADS, tm, HEAD_DIM), F32)
    return pl.pallas_call(
        _qkv_proj_kernel,
        grid=(m // tm,),
        in_specs=[pl.BlockSpec((tm, D_MODEL), lambda i: (i, 0)),
                  w_spec(0), w_spec(1), w_spec(2), table_spec, table_spec],
        out_specs=[pl.BlockSpec((None, slabs, tm, HEAD_DIM), lambda i: (i // nt, 0, i % nt, 0)),
                   dil_spec(4), dil_spec(16)],
        out_shape=[jax.ShapeDtypeStruct((batch, slabs, seq, HEAD_DIM), BF16),
                   dil_shape(4), dil_shape(16)],
        scratch_shapes=[head_scratch] * 6,
        compiler_params=_params(("arbitrary",)),
        name="qkv_proj",
    )(xn, w_in_bf16, w_in_bf16, w_in_bf16, cos_t, sin_t)


def _gmlp_body(xn, wu_ref, wv_ref, lng_ref, lnb_ref, ws_ref, bs_ref, out_ref):
    tm = xn.shape[0]
    a = jax.nn.gelu(jnp.dot(xn, wv_ref[...], preferred_element_type=F32))
    u = jax.nn.gelu(jnp.dot(xn, wu_ref[...], preferred_element_type=F32))
    row = lax.broadcasted_iota(jnp.int32, (CHUNK, CHUNK), 0)
    col = lax.broadcasted_iota(jnp.int32, (CHUNK, CHUNK), 1)
    causal = col <= row
    for g in range(GMLP_GROUPS):
        cols = slice(g * GMLP_GROUP_DIM, (g + 1) * GMLP_GROUP_DIM)
        ag = a[:, cols]
        mu = jnp.mean(ag, axis=-1, keepdims=True)
        xc = ag - mu
        var = jnp.mean(xc * xc, axis=-1, keepdims=True)
        vn = (xc * lax.rsqrt(var + LN_EPS) * lng_ref[:, cols] + lnb_ref[:, cols]).astype(BF16)
        w = jnp.where(causal, ws_ref[g], 0.0).astype(BF16)
        bias = bs_ref[:, g:g + 1]
        for c in range(tm // CHUNK):
            rows = slice(c * CHUNK, (c + 1) * CHUNK)
            sp = jnp.dot(w, vn[rows], preferred_element_type=F32) + bias
            out_ref[rows, cols] = (u[rows, cols] * sp).astype(out_ref.dtype)


def _gmlp_kernel(x_ref, g_ref, *refs, n_casts):
    body_refs, cast_srcs = refs[:6], refs[6:6 + n_casts]
    gm_ref, xn_ref, *cast_dsts = refs[6 + n_casts:]
    xn = _rms_normalize(x_ref[...], g_ref[...]).astype(BF16)
    xn_ref[...] = xn
    _gmlp_body(xn, *body_refs, gm_ref)
    _run_cast_riders(cast_srcs, cast_dsts)


def _gmlp(x2d, gain, w_in_bf16, ln_g, ln_b, w_spatial, b_spatial_t, casts):
    m = x2d.shape[0]
    tm = GMLP_TM
    n_steps = m // tm
    first_block = 3 * ATTN_WIDTH // GMLP_WIDTH
    row_spec = pl.BlockSpec((tm, D_MODEL), lambda i: (i, 0))
    vec_spec = pl.BlockSpec((1, GMLP_WIDTH), lambda i: (0, 0))
    in_specs = [row_spec,
                pl.BlockSpec((1, D_MODEL), lambda i: (0, 0)),
                _resident((D_MODEL, GMLP_WIDTH), lambda i: (0, first_block)),
                _resident((D_MODEL, GMLP_WIDTH), lambda i: (0, first_block + 1)),
                vec_spec, vec_spec,
                pl.BlockSpec((GMLP_GROUPS, CHUNK, CHUNK), lambda i: (0, 0, 0)),
                pl.BlockSpec((CHUNK, GMLP_GROUPS), lambda i: (0, 0))]
    operands = [x2d, gain, w_in_bf16, w_in_bf16, ln_g, ln_b, w_spatial, b_spatial_t]
    out_specs = [pl.BlockSpec((tm, GMLP_WIDTH), lambda i: (i, 0)), row_spec]
    out_shape = [jax.ShapeDtypeStruct((m, GMLP_WIDTH), BF16),
                 jax.ShapeDtypeStruct((m, D_MODEL), BF16)]
    for weight, layer in casts:
        operand, in_spec, out_spec, shape = _cast_rider(weight, layer, n_steps, lambda i: i)
        operands.append(operand)
        in_specs.append(in_spec)
        out_specs.append(out_spec)
        out_shape.append(shape)
    return pl.pallas_call(
        functools.partial(_gmlp_kernel, n_casts=len(casts)),
        grid=(n_steps,),
        in_specs=in_specs,
        out_specs=out_specs,
        out_shape=out_shape,
        compiler_params=_params(("arbitrary",)),
        name="gmlp",
    )(*operands)


def _attention_kernel(*refs, n_casts):
    (q1_ref, k1_ref, v1_ref, q4_ref, k4_ref, v4_ref, q16_ref, k16_ref, v16_ref) = refs[:9]
    cast_srcs, out_ref = refs[9:9 + n_casts], refs[9 + n_casts]
    cast_dsts = refs[10 + n_casts:10 + 2 * n_casts]
    acc_ref, max_ref, sum_ref = refs[10 + 2 * n_casts:]
    _run_cast_riders(cast_srcs, cast_dsts)
    seq = out_ref.shape[0]
    blk = ATTN_BLOCK
    exp2_scale = HEAD_DIM ** -0.5 * math.log2(math.e)
    row = lax.broadcasted_iota(jnp.int32, (blk, blk), 0)
    col = lax.broadcasted_iota(jnp.int32, (blk, blk), 1)
    cur_mask = col <= row
    prev_mask = col >= row
    nt_dims = (((1,), (1,)), ((), ()))
    ones = jnp.ones((2 * blk, HEAD_DIM), BF16)

    def attend(slot, d, q_ref, k_ref, v_ref, idx):
        nb = seq // (d * blk)
        n, r = idx % nb, idx // nb
        q = q_ref[pl.ds(idx * blk, blk), :]
        if n == 0:
            n_keys = blk
            keys = pl.ds(idx * blk, n_keys)
            s = lax.dot_general(q, k_ref[keys, :], nt_dims, preferred_element_type=F32)
            s = jnp.where(cur_mask, s, -jnp.inf)
            m = jnp.max(s, axis=-1, keepdims=True)
            p = jnp.exp2((s - m) * exp2_scale).astype(BF16)
        else:
            n_keys = 2 * blk
            keys = pl.ds((idx - 1) * blk, n_keys)
            s = lax.dot_general(q, k_ref[keys, :], nt_dims, preferred_element_type=F32)
            s_p = jnp.where(prev_mask, s[:, :blk], -jnp.inf)
            s_c = jnp.where(cur_mask, s[:, blk:], -jnp.inf)
            m = jnp.max(jnp.maximum(s_c, s_p), axis=-1, keepdims=True)
            p = jnp.concatenate([jnp.exp2((s_p - m) * exp2_scale).astype(BF16),
                                 jnp.exp2((s_c - m) * exp2_scale).astype(BF16)], axis=1)
        v_ext = jnp.concatenate([v_ref[keys, :], ones[:n_keys]], axis=1)
        acc = jnp.dot(p, v_ext, preferred_element_type=F32)
        a0, l0, m0 = acc[:, :HEAD_DIM], acc[:, HEAD_DIM:], jnp.broadcast_to(m, (blk, HEAD_DIM))
        if d == 1:
            rows = pl.ds(n * blk, blk)
            m1, m2 = max_ref[0, rows, :], max_ref[1, rows, :]
            top = jnp.maximum(jnp.maximum(m0, m1), m2)
            e0 = jnp.exp2((m0 - top) * exp2_scale)
            e1 = jnp.exp2((m1 - top) * exp2_scale)
            e2 = jnp.exp2((m2 - top) * exp2_scale)
            den = e0 * l0 + e1 * sum_ref[0, rows, :] + e2 * sum_ref[1, rows, :]
            num = e0 * a0 + e1 * acc_ref[0, rows, :] + e2 * acc_ref[1, rows, :]
            out_ref[rows, :] = (num / den).astype(out_ref.dtype)
            return
        if d <= MAX_SINGLE_ACCESS_STRIDE:
            rows = pl.ds(n * (blk * d) + r, blk, stride=d)
        else:
            slot = STAGING_SLOT
            rows = pl.ds((r % 4) * (seq // 4) + (n * blk * d + r) // 4, blk, stride=d // 4)
        acc_ref[slot, rows, :] = a0
        sum_ref[slot, rows, :] = l0
        max_ref[slot, rows, :] = m0

    dilated = ((0, 4, q4_ref, k4_ref, v4_ref), (1, 16, q16_ref, k16_ref, v16_ref))
    for slot, d, q_ref, k_ref, v_ref in dilated:
        for idx in range(seq // blk):
            attend(slot, d, q_ref, k_ref, v_ref, idx)
        if d > MAX_SINGLE_ACCESS_STRIDE:
            for ref in (acc_ref, sum_ref, max_ref):
                for r4 in range(4):
                    for part in range(seq // (4 * HOP_ROWS)):
                        u0 = part * HOP_ROWS
                        src = pl.ds(r4 * (seq // 4) + u0, HOP_ROWS)
                        ref[slot, pl.ds(4 * u0 + r4, HOP_ROWS, stride=4), :] = ref[STAGING_SLOT, src, :]
    for idx in range(seq // blk):
        attend(None, 1, q1_ref, k1_ref, v1_ref, idx)


def _attention(qkv_by_order, batch, seq, casts):
    def spec(t):
        return pl.BlockSpec((None, None, seq, HEAD_DIM),
                            lambda b, h: (b, t * ATTN_HEADS + h, 0, 0))

    in_specs = [spec(0), spec(1), spec(2)] * len(DILATIONS)
    operands = [a for a in qkv_by_order for _ in range(3)]
    out_specs = [pl.BlockSpec((None, seq, HEAD_DIM), lambda b, h: (b, 0, h))]
    out_shape = [jax.ShapeDtypeStruct((batch, seq, ATTN_WIDTH), BF16)]
    for weight, layer in casts:
        operand, in_spec, out_spec, shape = _cast_rider(
            weight, layer, batch * ATTN_HEADS, lambda b, h: b * ATTN_HEADS + h)
        operands.append(operand)
        in_specs.append(in_spec)
        out_specs.append(out_spec)
        out_shape.append(shape)
    stat = pltpu.VMEM((STAGING_SLOT + 1, seq, HEAD_DIM), F32)
    return pl.pallas_call(
        functools.partial(_attention_kernel, n_casts=len(casts)),
        grid=(batch, ATTN_HEADS),
        in_specs=in_specs,
        out_specs=out_specs,
        out_shape=out_shape,
        scratch_shapes=[stat, stat, stat],
        compiler_params=_params(("arbitrary", "arbitrary")),
        name="dilated_attention",
    )(*operands)


def _out_proj_kernel(attn_ref, gm_ref, w_ref, x_ref, g_ref, x1_ref, xn_ref):
    mix = jnp.concatenate([attn_ref[...], gm_ref[...]], axis=1)
    x1 = x_ref[...] + jnp.dot(mix, w_ref[...], preferred_element_type=F32)
    x1_ref[...] = x1
    xn_ref[...] = _rms_normalize(x1, g_ref[...]).astype(BF16)


def _out_proj(attn, gm, w_out_bf16, x2d, gain):
    m = x2d.shape[0]
    tm = OUT_TM
    row_spec = pl.BlockSpec((tm, D_MODEL), lambda i: (i, 0))
    return pl.pallas_call(
        _out_proj_kernel,
        grid=(m // tm,),
        in_specs=[
            pl.BlockSpec((tm, ATTN_WIDTH), lambda i: (i, 0)),
            pl.BlockSpec((tm, GMLP_WIDTH), lambda i: (i, 0)),
            _resident((ATTN_WIDTH + GMLP_WIDTH, D_MODEL), lambda i: (0, 0)),
            row_spec,
            pl.BlockSpec((1, D_MODEL), lambda i: (0, 0)),
        ],
        out_specs=[row_spec, row_spec],
        out_shape=[jax.ShapeDtypeStruct((m, D_MODEL), F32),
                   jax.ShapeDtypeStruct((m, D_MODEL), BF16)],
        compiler_params=_params(("arbitrary",)),
        name="out_proj",
    )(attn, gm, w_out_bf16, x2d, gain)


def _mlp_kernel(x_ref, xn_ref, wu_ref, wd_ref, *refs):
    *gain_refs, acc_ref = refs
    f = pl.program_id(1)

    @pl.when(f == 0)
    def _():
        acc_ref[...] = x_ref[...]

    h = jnp.dot(xn_ref[...], wu_ref[...], preferred_element_type=F32)
    h = jnp.square(jnp.maximum(h, 0.0)).astype(BF16)
    acc_ref[...] += jnp.dot(h, wd_ref[...], preferred_element_type=F32)

    if gain_refs:
        @pl.when(f == pl.num_programs(1) - 1)
        def _():
            acc_ref[...] = _rms_normalize(acc_ref[...], gain_refs[0][...])


def _mlp(x1, xn, w_up_bf16, w_down_bf16, final_gain=None):
    m = x1.shape[0]
    tm, tf = MLP_TM, MLP_TF
    row_spec = pl.BlockSpec((tm, D_MODEL), lambda i, f: (i, 0))
    in_specs = [row_spec, row_spec,
                pl.BlockSpec((D_MODEL, tf), lambda i, f: (0, f)),
                pl.BlockSpec((tf, D_MODEL), lambda i, f: (f, 0))]
    operands = [x1, xn, w_up_bf16, w_down_bf16]
    if final_gain is not None:
        in_specs.append(pl.BlockSpec((1, D_MODEL), lambda i, f: (0, 0)))
        operands.append(final_gain)
    return pl.pallas_call(
        _mlp_kernel,
        grid=(m // tm, D_FF // tf),
        in_specs=in_specs,
        out_specs=row_spec,
        out_shape=jax.ShapeDtypeStruct((m, D_MODEL), F32),
        compiler_params=_params(("arbitrary", "arbitrary")),
        name="mlp",
    )(*operands)


def _rotary_tables(seq):
    half = HEAD_DIM // 2
    inv_freq = ROPE_THETA ** (-jnp.arange(half, dtype=F32) / half)
    ang = jnp.arange(seq, dtype=jnp.int32).astype(F32)[:, None] * inv_freq[None, :]
    cos, sin = jnp.cos(ang), jnp.sin(ang)
    return jnp.concatenate([cos, cos], axis=-1), jnp.concatenate([-sin, sin], axis=-1)


def kernel(x, norm1_g, w_in, gmlp_ln_g, gmlp_ln_b, w_spatial, b_spatial, w_out,
           norm2_g, w_up, w_down, final_g):
    batch, seq, _ = x.shape
    depth = w_in.shape[0]
    m = batch * seq
    cos_t, sin_t = _rotary_tables(seq)
    gains1 = norm1_g.reshape(depth, 1, D_MODEL)
    gains2 = norm2_g.reshape(depth, 1, D_MODEL)
    final_gain = final_g.reshape(1, D_MODEL)

    w_in_bf16 = w_in[0].astype(BF16)

    x2d = x.reshape(m, D_MODEL)
    for l in range(depth):
        last_layer = l == depth - 1
        gmlp_casts = [(w_up, l)] + ([] if last_layer else [(w_in, l + 1)])
        gm, xn, w_up_bf16, *w_in_next = _gmlp(
            x2d, gains1[l], w_in_bf16, gmlp_ln_g[l].reshape(1, GMLP_WIDTH),
            gmlp_ln_b[l].reshape(1, GMLP_WIDTH), w_spatial[l], jnp.transpose(b_spatial[l]),
            gmlp_casts)
        qkv = [a.reshape(batch, 3 * ATTN_HEADS, seq, HEAD_DIM)
               for a in _qkv_proj(xn, w_in_bf16, cos_t, sin_t, batch, seq)]
        attn, w_down_bf16, w_out_bf16 = _attention(qkv, batch, seq, [(w_down, l), (w_out, l)])
        x1, xn2 = _out_proj(attn.reshape(m, ATTN_WIDTH), gm, w_out_bf16, x2d, gains2[l])
        x2d = _mlp(x1, xn2, w_up_bf16, w_down_bf16, final_gain if last_layer else None)
        if not last_layer:
            w_in_bf16, = w_in_next
    return x2d.reshape(batch, seq, D_MODEL)
```

```python
import functools
import math

import jax
import jax.numpy as jnp
from jax import lax
from jax.experimental import pallas as pl
from jax.experimental.pallas import tpu as pltpu

D_MODEL = 2048
HEAD_DIM = 128
ATTN_HEADS = 8
ATTN_WIDTH = ATTN_HEADS * HEAD_DIM
GMLP_GROUPS = 8
GMLP_GROUP_DIM = 128
GMLP_WIDTH = GMLP_GROUPS * GMLP_GROUP_DIM
CHUNK = 128
ATTN_BLOCK = 128
DILATIONS = (1, 4, 16)
ROPE_THETA = 10000.0
D_FF = 4 * D_MODEL
NORM_EPS = 1e-6
LN_EPS = 1e-5

V7X_VMEM_LIMIT_BYTES = 56 * 1024 * 1024

PROJ_TM = 512
GMLP_TM = 512
MLP_TM = 512
MLP_TF = 1024
HOP_ROWS = 512
MAX_SINGLE_ACCESS_STRIDE = 4
STAGING_SLOT = len(DILATIONS) - 1

F32 = jnp.float32
BF16 = jnp.bfloat16


def _rms_normalize(x, gain):
    return x * lax.rsqrt(jnp.mean(x * x, axis=-1, keepdims=True) + NORM_EPS) * gain


def _params(semantics):
    return pltpu.CompilerParams(dimension_semantics=semantics,
                                vmem_limit_bytes=V7X_VMEM_LIMIT_BYTES)


def _resident(shape, index_map):
    return pl.BlockSpec(shape, index_map, pipeline_mode=pl.Buffered(1))


def _cast_rider(stacked_f32, layer, n_steps, step_of):
    _, rows, cols = stacked_f32.shape
    chunk = rows // n_steps
    in_spec = pl.BlockSpec((None, chunk, cols), lambda *ids: (layer, step_of(*ids), 0))
    out_spec = pl.BlockSpec((chunk, cols), lambda *ids: (step_of(*ids), 0))
    return stacked_f32, in_spec, out_spec, jax.ShapeDtypeStruct((rows, cols), BF16)


def _run_cast_riders(src_refs, dst_refs):
    for src, dst in zip(src_refs, dst_refs):
        dst[...] = src[...].astype(BF16)


def _qkv_proj_kernel(xn_ref, wq_ref, wk_ref, wv_ref, cos_ref, sin_ref,
                     nat_ref, d4_ref, d16_ref, *stage_refs):
    tm = xn_ref.shape[0]
    quarter = tm // 4
    xn = xn_ref[...]
    jobs = ((wq_ref, True), (wk_ref, True), (wv_ref, False))
    for t, (w_ref, rotate) in enumerate(jobs):
        head_ref, head4_ref = stage_refs[2 * t], stage_refs[2 * t + 1]
        z = jnp.dot(xn, w_ref[...], preferred_element_type=F32)
        for h in range(ATTN_HEADS):
            y = z[:, h * HEAD_DIM:(h + 1) * HEAD_DIM]
            if rotate:
                y = y * cos_ref[...] + pltpu.roll(y, HEAD_DIM // 2, 1) * sin_ref[...]
            slab = t * ATTN_HEADS + h
            nat_ref[slab] = y.astype(BF16)
            head_ref[h] = y
            for r4 in range(4):
                y4 = head_ref[h, pl.ds(r4, quarter, stride=4), :]
                d4_ref[slab, r4] = y4.astype(BF16)
                head4_ref[h, r4 * quarter:(r4 + 1) * quarter, :] = y4
            for r16 in range(16):
                c, r4 = divmod(r16, 4)
                y16 = head4_ref[h, pl.ds(r4 * quarter + c, tm // 16, stride=4), :]
                d16_ref[slab, r16] = y16.astype(BF16)


def _qkv_proj(xn, w_in_bf16, cos_t, sin_t, batch, seq):
    m = xn.shape[0]
    tm = PROJ_TM
    nt = seq // tm
    slabs = 3 * ATTN_HEADS

    def dil_shape(d):
        return jax.ShapeDtypeStruct((batch, slabs, d, seq // d, HEAD_DIM), BF16)

    def dil_spec(d):
        return pl.BlockSpec((None, slabs, d, tm // d, HEAD_DIM),
                            lambda i: (i // nt, 0, 0, i % nt, 0))

    def w_spec(col_block):
        return _resident((D_MODEL, ATTN_WIDTH), lambda i: (0, col_block))

    table_spec = pl.BlockSpec((tm, HEAD_DIM), lambda i: (i % nt, 0))
    head_scratch = pltpu.VMEM((ATTN_HEADS, tm, HEAD_DIM), F32)
    return pl.pallas_call(
        _qkv_proj_kernel,
        grid=(m // tm,),
        in_specs=[pl.BlockSpec((tm, D_MODEL), lambda i: (i, 0)),
                  w_spec(0), w_spec(1), w_spec(2), table_spec, table_spec],
        out_specs=[pl.BlockSpec((None, slabs, tm, HEAD_DIM), lambda i: (i // nt, 0, i % nt, 0)),
                   dil_spec(4), dil_spec(16)],
        out_shape=[jax.ShapeDtypeStruct((batch, slabs, seq, HEAD_DIM), BF16),
                   dil_shape(4), dil_shape(16)],
        scratch_shapes=[head_scratch] * 6,
        compiler_params=_params(("arbitrary",)),
        name="qkv_proj",
    )(xn, w_in_bf16, w_in_bf16, w_in_bf16, cos_t, sin_t)


def _gmlp_body(xn, wu_ref, wv_ref, lng_ref, lnb_ref, ws_ref, bs_ref, out_ref):
    tm = xn.shape[0]
    a = jax.nn.gelu(jnp.dot(xn, wv_ref[...], preferred_element_type=F32))
    u = jax.nn.gelu(jnp.dot(xn, wu_ref[...], preferred_element_type=F32))
    row = lax.broadcasted_iota(jnp.int32, (CHUNK, CHUNK), 0)
    col = lax.broadcasted_iota(jnp.int32, (CHUNK, CHUNK), 1)
    causal = col <= row
    for g in range(GMLP_GROUPS):
        cols = slice(g * GMLP_GROUP_DIM, (g + 1) * GMLP_GROUP_DIM)
        ag = a[:, cols]
        mu = jnp.mean(ag, axis=-1, keepdims=True)
        xc = ag - mu
        var = jnp.mean(xc * xc, axis=-1, keepdims=True)
        vn = (xc * lax.rsqrt(var + LN_EPS) * lng_ref[:, cols] + lnb_ref[:, cols]).astype(BF16)
        w = jnp.where(causal, ws_ref[g], 0.0).astype(BF16)
        bias = bs_ref[:, g:g + 1]
        for c in range(tm // CHUNK):
            rows = slice(c * CHUNK, (c + 1) * CHUNK)
            sp = jnp.dot(w, vn[rows], preferred_element_type=F32) + bias
            out_ref[rows, cols] = (u[rows, cols] * sp).astype(out_ref.dtype)


def _gmlp_kernel(*refs, fuse_norm, n_casts):
    n_in = (2 if fuse_norm else 1) + 6
    ins, cast_srcs, outs = refs[:n_in], refs[n_in:n_in + n_casts], refs[n_in + n_casts:]
    cast_dsts = outs[len(outs) - n_casts:]
    if fuse_norm:
        x_ref, g_ref, *body_refs = ins
        xn = _rms_normalize(x_ref[...], g_ref[...]).astype(BF16)
        outs[1][...] = xn
    else:
        xn_ref, *body_refs = ins
        xn = xn_ref[...]
    _gmlp_body(xn, *body_refs, outs[0])
    _run_cast_riders(cast_srcs, cast_dsts)


def _gmlp(x_or_xn, gain, w_in_bf16, ln_g, ln_b, w_spatial, b_spatial_t, casts):
    m = x_or_xn.shape[0]
    tm = GMLP_TM
    n_steps = m // tm
    first_block = 3 * ATTN_WIDTH // GMLP_WIDTH
    row_spec = pl.BlockSpec((tm, D_MODEL), lambda i: (i, 0))
    vec_spec = pl.BlockSpec((1, GMLP_WIDTH), lambda i: (0, 0))
    in_specs = [row_spec]
    operands = [x_or_xn]
    out_specs = [pl.BlockSpec((tm, GMLP_WIDTH), lambda i: (i, 0))]
    out_shape = [jax.ShapeDtypeStruct((m, GMLP_WIDTH), BF16)]
    if gain is not None:
        in_specs.append(pl.BlockSpec((1, D_MODEL), lambda i: (0, 0)))
        operands.append(gain)
        out_specs.append(row_spec)
        out_shape.append(jax.ShapeDtypeStruct((m, D_MODEL), BF16))
    in_specs += [_resident((D_MODEL, GMLP_WIDTH), lambda i: (0, first_block)),
                 _resident((D_MODEL, GMLP_WIDTH), lambda i: (0, first_block + 1)),
                 vec_spec, vec_spec,
                 pl.BlockSpec((GMLP_GROUPS, CHUNK, CHUNK), lambda i: (0, 0, 0)),
                 pl.BlockSpec((CHUNK, GMLP_GROUPS), lambda i: (0, 0))]
    operands += [w_in_bf16, w_in_bf16, ln_g, ln_b, w_spatial, b_spatial_t]
    for weight, layer in casts:
        operand, in_spec, out_spec, shape = _cast_rider(weight, layer, n_steps, lambda i: i)
        operands.append(operand)
        in_specs.append(in_spec)
        out_specs.append(out_spec)
        out_shape.append(shape)
    return pl.pallas_call(
        functools.partial(_gmlp_kernel, fuse_norm=gain is not None, n_casts=len(casts)),
        grid=(n_steps,),
        in_specs=in_specs,
        out_specs=out_specs,
        out_shape=out_shape,
        compiler_params=_params(("arbitrary",)),
        name="gmlp",
    )(*operands)


def _attention_kernel(*refs, n_casts):
    (q1_ref, k1_ref, v1_ref, q4_ref, k4_ref, v4_ref, q16_ref, k16_ref, v16_ref) = refs[:9]
    cast_srcs, out_ref = refs[9:9 + n_casts], refs[9 + n_casts]
    cast_dsts = refs[10 + n_casts:10 + 2 * n_casts]
    acc_ref, max_ref, sum_ref = refs[10 + 2 * n_casts:]
    _run_cast_riders(cast_srcs, cast_dsts)
    seq = out_ref.shape[0]
    blk = ATTN_BLOCK
    exp2_scale = HEAD_DIM ** -0.5 * math.log2(math.e)
    row = lax.broadcasted_iota(jnp.int32, (blk, blk), 0)
    col = lax.broadcasted_iota(jnp.int32, (blk, blk), 1)
    cur_mask = col <= row
    prev_mask = col >= row
    nt_dims = (((1,), (1,)), ((), ()))
    ones = jnp.ones((2 * blk, HEAD_DIM), BF16)

    def attend(slot, d, q_ref, k_ref, v_ref, idx):
        nb = seq // (d * blk)
        n, r = idx % nb, idx // nb
        q = q_ref[pl.ds(idx * blk, blk), :]
        if n == 0:
            n_keys = blk
            keys = pl.ds(idx * blk, n_keys)
            s = lax.dot_general(q, k_ref[keys, :], nt_dims, preferred_element_type=F32)
            s = jnp.where(cur_mask, s, -jnp.inf)
            m = jnp.max(s, axis=-1, keepdims=True)
            p = jnp.exp2((s - m) * exp2_scale).astype(BF16)
        else:
            n_keys = 2 * blk
            keys = pl.ds((idx - 1) * blk, n_keys)
            s = lax.dot_general(q, k_ref[keys, :], nt_dims, preferred_element_type=F32)
            s_p = jnp.where(prev_mask, s[:, :blk], -jnp.inf)
            s_c = jnp.where(cur_mask, s[:, blk:], -jnp.inf)
            m = jnp.max(jnp.maximum(s_c, s_p), axis=-1, keepdims=True)
            p = jnp.concatenate([jnp.exp2((s_p - m) * exp2_scale).astype(BF16),
                                 jnp.exp2((s_c - m) * exp2_scale).astype(BF16)], axis=1)
        v_ext = jnp.concatenate([v_ref[keys, :], ones[:n_keys]], axis=1)
        acc = jnp.dot(p, v_ext, preferred_element_type=F32)
        a0, l0, m0 = acc[:, :HEAD_DIM], acc[:, HEAD_DIM:], jnp.broadcast_to(m, (blk, HEAD_DIM))
        if d == 1:
            rows = pl.ds(n * blk, blk)
            m1, m2 = max_ref[0, rows, :], max_ref[1, rows, :]
            top = jnp.maximum(jnp.maximum(m0, m1), m2)
            e0 = jnp.exp2((m0 - top) * exp2_scale)
            e1 = jnp.exp2((m1 - top) * exp2_scale)
            e2 = jnp.exp2((m2 - top) * exp2_scale)
            den = e0 * l0 + e1 * sum_ref[0, rows, :] + e2 * sum_ref[1, rows, :]
            num = e0 * a0 + e1 * acc_ref[0, rows, :] + e2 * acc_ref[1, rows, :]
            out_ref[rows, :] = (num / den).astype(out_ref.dtype)
            return
        if d <= MAX_SINGLE_ACCESS_STRIDE:
            rows = pl.ds(n * (blk * d) + r, blk, stride=d)
        else:
            slot = STAGING_SLOT
            rows = pl.ds((r % 4) * (seq // 4) + (n * blk * d + r) // 4, blk, stride=d // 4)
        acc_ref[slot, rows, :] = a0
        sum_ref[slot, rows, :] = l0
        max_ref[slot, rows, :] = m0

    dilated = ((0, 4, q4_ref, k4_ref, v4_ref), (1, 16, q16_ref, k16_ref, v16_ref))
    for slot, d, q_ref, k_ref, v_ref in dilated:
        for idx in range(seq // blk):
            attend(slot, d, q_ref, k_ref, v_ref, idx)
        if d > MAX_SINGLE_ACCESS_STRIDE:
            for ref in (acc_ref, sum_ref, max_ref):
                for r4 in range(4):
                    for part in range(seq // (4 * HOP_ROWS)):
                        u0 = part * HOP_ROWS
                        src = pl.ds(r4 * (seq // 4) + u0, HOP_ROWS)
                        ref[slot, pl.ds(4 * u0 + r4, HOP_ROWS, stride=4), :] = ref[STAGING_SLOT, src, :]
    for idx in range(seq // blk):
        attend(None, 1, q1_ref, k1_ref, v1_ref, idx)


def _attention(qkv_by_order, batch, seq, casts):
    def spec(t):
        return pl.BlockSpec((None, None, seq, HEAD_DIM),
                            lambda b, h: (b, t * ATTN_HEADS + h, 0, 0))

    in_specs = [spec(0), spec(1), spec(2)] * len(DILATIONS)
    operands = [a for a in qkv_by_order for _ in range(3)]
    out_specs = [pl.BlockSpec((None, seq, HEAD_DIM), lambda b, h: (b, 0, h))]
    out_shape = [jax.ShapeDtypeStruct((batch, seq, ATTN_WIDTH), BF16)]
    for weight, layer in casts:
        operand, in_spec, out_spec, shape = _cast_rider(
            weight, layer, batch * ATTN_HEADS, lambda b, h: b * ATTN_HEADS + h)
        operands.append(operand)
        in_specs.append(in_spec)
        out_specs.append(out_spec)
        out_shape.append(shape)
    stat = pltpu.VMEM((STAGING_SLOT + 1, seq, HEAD_DIM), F32)
    return pl.pallas_call(
        functools.partial(_attention_kernel, n_casts=len(casts)),
        grid=(batch, ATTN_HEADS),
        in_specs=in_specs,
        out_specs=out_specs,
        out_shape=out_shape,
        scratch_shapes=[stat, stat, stat],
        compiler_params=_params(("arbitrary", "arbitrary")),
        name="dilated_attention",
    )(*operands)


def _out_mlp_kernel(x_ref, attn_ref, gm_ref, wo_ref, g2_ref, wu_ref, wd_ref, g_ref,
                    *refs, last_layer):
    *out_refs, xn_ref = refs
    acc_ref = out_refs[0]
    f = pl.program_id(1)

    @pl.when(f == 0)
    def _():
        mix = jnp.concatenate([attn_ref[...], gm_ref[...]], axis=1)
        x1 = x_ref[...] + jnp.dot(mix, wo_ref[...], preferred_element_type=F32)
        acc_ref[...] = x1
        xn_ref[...] = _rms_normalize(x1, g2_ref[...]).astype(BF16)

    h = jnp.dot(xn_ref[...], wu_ref[...], preferred_element_type=F32)
    h = jnp.square(jnp.maximum(h, 0.0)).astype(BF16)
    acc_ref[...] += jnp.dot(h, wd_ref[...], preferred_element_type=F32)

    @pl.when(f == pl.num_programs(1) - 1)
    def _():
        normed = _rms_normalize(acc_ref[...], g_ref[...])
        if last_layer:
            acc_ref[...] = normed
        else:
            out_refs[1][...] = normed.astype(BF16)


def _out_mlp(x2d, attn, gm, w_out_bf16, gain2, w_up_bf16, w_down_bf16, next_gain, last_layer):
    m = x2d.shape[0]
    tm, tf = MLP_TM, MLP_TF
    row_spec = pl.BlockSpec((tm, D_MODEL), lambda i, f: (i, 0))
    gain_spec = pl.BlockSpec((1, D_MODEL), lambda i, f: (0, 0))
    out_specs = [row_spec]
    out_shape = [jax.ShapeDtypeStruct((m, D_MODEL), F32)]
    if not last_layer:
        out_specs.append(row_spec)
        out_shape.append(jax.ShapeDtypeStruct((m, D_MODEL), BF16))
    return pl.pallas_call(
        functools.partial(_out_mlp_kernel, last_layer=last_layer),
        grid=(m // tm, D_FF // tf),
        in_specs=[
            row_spec,
            pl.BlockSpec((tm, ATTN_WIDTH), lambda i, f: (i, 0)),
            pl.BlockSpec((tm, GMLP_WIDTH), lambda i, f: (i, 0)),
            _resident((ATTN_WIDTH + GMLP_WIDTH, D_MODEL), lambda i, f: (0, 0)),
            gain_spec,
            pl.BlockSpec((D_MODEL, tf), lambda i, f: (0, f)),
            pl.BlockSpec((tf, D_MODEL), lambda i, f: (f, 0)),
            gain_spec,
        ],
        out_specs=out_specs,
        out_shape=out_shape,
        scratch_shapes=[pltpu.VMEM((tm, D_MODEL), BF16)],
        compiler_params=_params(("arbitrary", "arbitrary")),
        name="out_mlp",
    )(x2d, attn, gm, w_out_bf16, gain2, w_up_bf16, w_down_bf16, next_gain)


def _rotary_tables(seq):
    half = HEAD_DIM // 2
    inv_freq = ROPE_THETA ** (-jnp.arange(half, dtype=F32) / half)
    ang = jnp.arange(seq, dtype=jnp.int32).astype(F32)[:, None] * inv_freq[None, :]
    cos, sin = jnp.cos(ang), jnp.sin(ang)
    return jnp.concatenate([cos, cos], axis=-1), jnp.concatenate([-sin, sin], axis=-1)


def kernel(x, norm1_g, w_in, gmlp_ln_g, gmlp_ln_b, w_spatial, b_spatial, w_out,
           norm2_g, w_up, w_down, final_g):
    batch, seq, _ = x.shape
    depth = w_in.shape[0]
    m = batch * seq
    cos_t, sin_t = _rotary_tables(seq)
    gains1 = norm1_g.reshape(depth, 1, D_MODEL)
    gains2 = norm2_g.reshape(depth, 1, D_MODEL)
    final_gain = final_g.reshape(1, D_MODEL)

    w_in_bf16 = w_in[0].astype(BF16)

    x2d = x.reshape(m, D_MODEL)
    xn = None
    for l in range(depth):
        last_layer = l == depth - 1
        gmlp_casts = [(w_up, l)] + ([] if last_layer else [(w_in, l + 1)])
        gm, *rest = _gmlp(x2d if xn is None else xn, gains1[l] if xn is None else None,
                          w_in_bf16, gmlp_ln_g[l].reshape(1, GMLP_WIDTH),
                          gmlp_ln_b[l].reshape(1, GMLP_WIDTH), w_spatial[l],
                          jnp.transpose(b_spatial[l]), gmlp_casts)
        if xn is None:
            xn, *rest = rest
        w_up_bf16, *w_in_next = rest
        qkv = [a.reshape(batch, 3 * ATTN_HEADS, seq, HEAD_DIM)
               for a in _qkv_proj(xn, w_in_bf16, cos_t, sin_t, batch, seq)]
        attn, w_down_bf16, w_out_bf16 = _attention(qkv, batch, seq, [(w_down, l), (w_out, l)])
        next_gain = final_gain if last_layer else gains1[l + 1]
        res = _out_mlp(x2d, attn.reshape(m, ATTN_WIDTH), gm, w_out_bf16, gains2[l],
                       w_up_bf16, w_down_bf16, next_gain, last_layer)
        if last_layer:
            x2d = res[0]
        else:
            x2d, xn = res
            w_in_bf16, = w_in_next
    return x2d.reshape(batch, seq, D_MODEL)
```
